```python
import math
import jax, jax.numpy as jnp
from jax import lax
import numpy as np

D_MODEL = 1024
BATCH = 32
SEQ = 2048
DEPTH = 1

SB_HEADS = 8
SB_HEAD_DIM = 64
SB_WIDTH = SB_HEADS * SB_HEAD_DIM
Q_BLOCK = 128
RW_HEADS = 8
RW_HEAD_DIM = 64
RW_WIDTH = RW_HEADS * RW_HEAD_DIM
DECAY_LORA = 64
ICLR_LORA = 64
GATE_LORA = 128
GN_EPS = 64e-5
D_FF = 4 * D_MODEL
PLE_DIM = 256
NORM_EPS = 1e-6

SB_COLS = 3 * SB_WIDTH
RW_COLS = 3 * RW_WIDTH + DECAY_LORA + ICLR_LORA + GATE_LORA
GATE_COLS = 2 * D_MODEL
IN_COLS = SB_COLS + RW_COLS + GATE_COLS

kernel_name = "hybrid_stickbreak_rwkv7_gated_block"


def rms_norm(x, g):
    xf = x.astype(jnp.float32)
    y = xf * lax.rsqrt(jnp.mean(xf * xf, axis=-1, keepdims=True) + NORM_EPS)
    return (y * g.astype(jnp.float32)).astype(x.dtype)


def stick_breaking_attention(q, k, v):
    q = jnp.transpose(q, (0, 2, 1, 3))
    k = jnp.transpose(k, (0, 2, 1, 3))
    v = jnp.transpose(v, (0, 2, 1, 3))
    seq = q.shape[2]
    scale = 1.0 / math.sqrt(q.shape[-1])
    outs = []
    for blk in range(seq // Q_BLOCK):
        q0 = blk * Q_BLOCK
        kend = q0 + Q_BLOCK
        z = jnp.einsum('bhqd,bhkd->bhqk', q[:, :, q0:kend], k[:, :, :kend]).astype(jnp.float32) * scale
        t_pos = q0 + jnp.arange(Q_BLOCK)[:, None]
        s_pos = jnp.arange(kend)[None, :]
        mask = s_pos < t_pos
        log_om = jnp.where(mask, jax.nn.log_sigmoid(-z), 0.0)
        rev = lax.cumsum(log_om, axis=log_om.ndim - 1, reverse=True)
        log_w = jax.nn.log_sigmoid(z) + rev - log_om
        w = jnp.where(mask, jnp.exp(log_w), 0.0)
        outs.append(jnp.einsum('bhqk,bhkd->bhqd', w.astype(v.dtype), v[:, :, :kend]))
    o = jnp.concatenate(outs, axis=2)
    return jnp.transpose(o, (0, 2, 1, 3))


def rwkv7_recurrence(r, w, k, v, a, b):
    bsz, _, heads, n = r.shape

    def step(state, inp):
        r_t, w_t, k_t, v_t, a_t, b_t = inp
        sa = jnp.einsum('bhvk,bhk->bhv', state, a_t)
        state = state * w_t[:, :, None, :] + sa[..., None] * b_t[:, :, None, :] + v_t[..., None] * k_t[:, :, None, :]
        y = jnp.einsum('bhvk,bhk->bhv', state, r_t)
        return state, y

    state0 = jnp.zeros((bsz, heads, n, n), jnp.float32)
    xs = tuple(jnp.moveaxis(t, 1, 0) for t in (r, w, k, v, a, b))
    _, ys = lax.scan(step, state0, xs)
    return jnp.moveaxis(ys, 0, 1)


def rwkv7_time_mix(u, shift_mu, w0, w2, a0, a2, g2, k_k, k_a, r_k, ln_w, ln_b):
    bsz, seq, _ = u.shape
    prev = jnp.pad(u, ((0, 0), (1, 0), (0, 0)))[:, :-1]
    u = u + (prev - u) * shift_mu
    r, k, v, xw, xa, xg = jnp.split(
        u, [RW_WIDTH, 2 * RW_WIDTH, 3 * RW_WIDTH, 3 * RW_WIDTH + DECAY_LORA,
            3 * RW_WIDTH + DECAY_LORA + ICLR_LORA], axis=-1)
    f32 = jnp.float32
    w_log = -jax.nn.softplus(-(w0 + jnp.tanh(xw) @ w2).astype(f32)) - 0.5
    decay = jnp.exp(-jnp.exp(w_log))
    a = jax.nn.sigmoid((a0 + xa @ a2).astype(f32))
    g = jax.nn.sigmoid(xg) @ g2
    hs = lambda t: t.astype(f32).reshape(bsz, seq, RW_HEADS, RW_HEAD_DIM)
    kk = hs(k * k_k)
    kk = kk / jnp.maximum(jnp.sqrt(jnp.sum(kk * kk, axis=-1, keepdims=True)), 1e-12)
    k_eff = hs(k.astype(f32) * (1.0 + (a - 1.0) * k_a.astype(f32)))
    a_h = a.reshape(bsz, seq, RW_HEADS, RW_HEAD_DIM)
    r_h, v_h = hs(r), hs(v)
    y = rwkv7_recurrence(r_h, decay.reshape(bsz, seq, RW_HEADS, RW_HEAD_DIM), k_eff, v_h, -kk, kk * a_h)
    mu = jnp.mean(y, axis=-1, keepdims=True)
    var = jnp.mean(jnp.square(y - mu), axis=-1, keepdims=True)
    y = ((y - mu) * lax.rsqrt(var + GN_EPS)).reshape(bsz, seq, RW_WIDTH)
    y = y * ln_w.astype(f32) + ln_b.astype(f32)
    bonus = jnp.sum(r_h * k_eff * r_k.astype(f32), axis=-1, keepdims=True) * v_h
    y = y + bonus.reshape(bsz, seq, RW_WIDTH)
    return (y * g.astype(f32)).astype(u.dtype)


def setup_inputs(seed: int = 0) -> dict:
    key = jax.random.key(seed)
    ks = jax.random.split(key, 26)
    nrm = lambda k, shape, fan_in: jax.random.normal(k, shape, jnp.float32) * (fan_in ** -0.5)
    gain = lambda k, shape: 1.0 + 0.02 * jax.random.normal(k, shape, jnp.float32)
    L = DEPTH
    return {
        "x": jax.random.normal(ks[0], (BATCH, SEQ, D_MODEL), jnp.float32),
        "p": jax.random.normal(ks[1], (DEPTH, BATCH, SEQ, PLE_DIM), jnp.float32),
        "attn_norm_g": gain(ks[2], (L, D_MODEL)),
        "w_in": nrm(ks[3], (L, D_MODEL, IN_COLS), D_MODEL),
        "shift_mu": jax.random.uniform(ks[4], (L, RW_COLS), jnp.float32, 0.0, 1.0),
        "decay_w0": jax.random.uniform(ks[5], (L, RW_WIDTH), jnp.float32, -3.0, 1.0),
        "decay_w2": 0.5 * nrm(ks[6], (L, DECAY_LORA, RW_WIDTH), DECAY_LORA),
        "iclr_a0": 0.1 * jax.random.normal(ks[7], (L, RW_WIDTH), jnp.float32),
        "iclr_a2": 0.5 * nrm(ks[8], (L, ICLR_LORA, RW_WIDTH), ICLR_LORA),
        "gate_g2": nrm(ks[9], (L, GATE_LORA, RW_WIDTH), GATE_LORA),
        "k_k": 0.85 + 0.02 * jax.random.normal(ks[10], (L, RW_WIDTH), jnp.float32),
        "k_a": gain(ks[11], (L, RW_WIDTH)),
        "r_k": 0.1 * jax.random.normal(ks[12], (L, RW_HEADS, RW_HEAD_DIM), jnp.float32),
        "ln_x_w": gain(ks[13], (L, RW_WIDTH)),
        "ln_x_b": 0.01 * jax.random.normal(ks[14], (L, RW_WIDTH), jnp.float32),
        "w_up_sb": nrm(ks[15], (L, SB_WIDTH, D_MODEL), SB_WIDTH),
        "w_up_rw": nrm(ks[16], (L, RW_WIDTH, D_MODEL), RW_WIDTH),
        "w_out": nrm(ks[17], (L, D_MODEL, D_MODEL), D_MODEL),
        "mlp_norm_g": gain(ks[18], (L, D_MODEL)),
        "w_ff1": nrm(ks[19], (L, D_MODEL, D_FF), D_MODEL),
        "w_ff2": nrm(ks[20], (L, D_FF, D_MODEL), D_FF),
        "ple_norm_g": gain(ks[21], (L, D_MODEL)),
        "w_ple_gate": nrm(ks[22], (L, D_MODEL, D_MODEL), D_MODEL),
        "w_ple_proj": nrm(ks[23], (L, PLE_DIM, D_MODEL), PLE_DIM),
        "final_norm_g": gain(ks[24], (D_MODEL,)),
    }


def reference(x, p, attn_norm_g, w_in, shift_mu, decay_w0, decay_w2, iclr_a0, iclr_a2, gate_g2,
              k_k, k_a, r_k, ln_x_w, ln_x_b, w_up_sb, w_up_rw, w_out, mlp_norm_g, w_ff1, w_ff2,
              ple_norm_g, w_ple_gate, w_ple_proj, final_norm_g):
    bsz, seq, _ = x.shape
    for i in range(DEPTH):
        h = rms_norm(x, attn_norm_g[i])
        u = h @ w_in[i]
        u_sb, u_rw, u_gate = jnp.split(u, [SB_COLS, SB_COLS + RW_COLS], axis=-1)
        q, k, v = jnp.split(u_sb.reshape(bsz, seq, 3 * SB_HEADS, SB_HEAD_DIM), 3, axis=2)
        o_sb = stick_breaking_attention(q, k, v).reshape(bsz, seq, SB_WIDTH)
        o_rw = rwkv7_time_mix(u_rw, shift_mu[i], decay_w0[i], decay_w2[i], iclr_a0[i], iclr_a2[i],
                              gate_g2[i], k_k[i], k_a[i], r_k[i], ln_x_w[i], ln_x_b[i])
        g_sb, g_rw = jnp.split(jax.nn.sigmoid(u_gate), 2, axis=-1)
        merged = g_sb * (o_sb @ w_up_sb[i]) + g_rw * (o_rw @ w_up_rw[i])
        x = x + merged @ w_out[i]
        h = rms_norm(x, mlp_norm_g[i])
        x = x + jnp.square(jax.nn.relu(h @ w_ff1[i])) @ w_ff2[i]
        gate = jax.nn.sigmoid(rms_norm(x, ple_norm_g[i]) @ w_ple_gate[i])
        x = x + gate * (p[i] @ w_ple_proj[i])
    return rms_norm(x, final_norm_g)
```

```python
import functools
import math

import jax
import jax.numpy as jnp
from jax import lax
from jax.experimental import pallas as pl
from jax.experimental.pallas import tpu as pltpu

NORM_EPS = 1e-6
GN_EPS = 64e-5
HEAD_DIM = 64
LANES = 128
Q_BLOCK = 128
RW_CHUNK = 64
DECAY_LORA = 64
ICLR_LORA = 64
GATE_LORA = 128
VMEM_LIMIT = 56 * 1024 * 1024

F32 = jnp.float32
BF16 = jnp.bfloat16

_NT = (((1,), (1,)), ((), ()))
_TN = (((0,), (0,)), ((), ()))


def _dot(a, b):
    return jnp.dot(a, b, preferred_element_type=F32)


def _dot_nt(a, b):
    return lax.dot_general(a, b, _NT, preferred_element_type=F32)


def _split2(x):
    hi = x.astype(BF16)
    lo = (x - hi.astype(F32)).astype(BF16)
    return hi, lo


def _split3(x):
    hi = x.astype(BF16)
    r1 = x - hi.astype(F32)
    mid = r1.astype(BF16)
    lo = (r1 - mid.astype(F32)).astype(BF16)
    return hi, mid, lo


def _rms(x, g):
    return x * lax.rsqrt(jnp.mean(x * x, axis=-1, keepdims=True) + NORM_EPS) * g


def _softplus(x):
    return jnp.maximum(x, 0.0) + jnp.log1p(jnp.exp(-jnp.abs(x)))


def _resident(shape):
    nd = len(shape)
    return pl.BlockSpec(shape, lambda *_: (0,) * nd, pipeline_mode=pl.Buffered(1))


def _inproj_kernel(x_ref, g_ref, wsb_ref, wrkv_ref, wlora_ref, wgate_ref,
                   sb_ref, rkv_ref, lora_ref, gate_ref):
    hb = _rms(x_ref[...], g_ref[...]).astype(BF16)
    sb_ref[...] = _dot(hb, wsb_ref[...]).astype(sb_ref.dtype)
    rkv_ref[...] = _dot(hb, wrkv_ref[...])
    lora_ref[...] = _dot(hb, wlora_ref[...])
    gate_ref[...] = jax.nn.sigmoid(_dot(hb, wgate_ref[...])).astype(gate_ref.dtype)


def _inproj(x2, g, wsb, wrkv, wlora, wgate, tm):
    t, d = x2.shape
    row = lambda n: pl.BlockSpec((tm, n), lambda i: (i, 0))
    ws = (wsb, wrkv, wlora, wgate)
    return pl.pallas_call(
        _inproj_kernel,
        grid=(t // tm,),
        in_specs=[row(d), _resident(g.shape)] + [_resident(w.shape) for w in ws],
        out_specs=[row(w.shape[1]) for w in ws],
        out_shape=[jax.ShapeDtypeStruct((t, wsb.shape[1]), BF16),
                   jax.ShapeDtypeStruct((t, wrkv.shape[1]), F32),
                   jax.ShapeDtypeStruct((t, wlora.shape[1]), F32),
                   jax.ShapeDtypeStruct((t, wgate.shape[1]), BF16)],
        compiler_params=pltpu.CompilerParams(dimension_semantics=("parallel",),
                                             vmem_limit_bytes=VMEM_LIMIT),
        name="inproj",
    )(x2, g, *ws)


def _sb_attn_kernel(q_ref, k_ref, v_ref, o_ref, *, scale):
    qi = pl.program_id(2)
    q = q_ref[0] * scale
    lane = lax.broadcasted_iota(jnp.int32, (Q_BLOCK, LANES), 1)
    row = lax.broadcasted_iota(jnp.int32, (Q_BLOCK, Q_BLOCK), 0)
    col = lax.broadcasted_iota(jnp.int32, (Q_BLOCK, Q_BLOCK), 1)
    strict = col < row
    later = (row > col).astype(BF16)

    def key_block(qh, j, c, acc, on_diagonal):
        start = pl.multiple_of(j * Q_BLOCK, Q_BLOCK)
        kb = k_ref[0, pl.ds(start, Q_BLOCK), :]
        vb = v_ref[0, pl.ds(start, Q_BLOCK), :]
        z = _dot_nt(qh, kb)
        t = jnp.log1p(jnp.exp(-jnp.abs(z)))
        log_beta = jnp.minimum(z, 0.0) - t
        log_om = jnp.minimum(-z, 0.0) - t
        if on_diagonal:
            log_om = jnp.where(strict, log_om, 0.0)
        hi, lo = _split2(log_om)
        after = _dot(hi, later) + _dot(lo, later)
        w = jnp.exp(log_beta + after + c)
        if on_diagonal:
            w = jnp.where(strict, w, 0.0)
        acc = acc + _dot(w.astype(BF16), vb)
        c = c + after[:, 0:1] + log_om[:, 0:1]
        return c, acc

    outs = []
    for head in range(LANES // HEAD_DIM):
        qh = jnp.where(lane // HEAD_DIM == head, q, jnp.zeros_like(q))
        c = jnp.zeros((Q_BLOCK, 1), F32)
        acc = jnp.zeros((Q_BLOCK, LANES), F32)
        c, acc = key_block(qh, qi, c, acc, True)
        c, acc = lax.fori_loop(
            0, qi, lambda i, ca, qh=qh: key_block(qh, qi - 1 - i, ca[0], ca[1], False), (c, acc))
        outs.append(acc)
    o_ref[0] = jnp.where(lane < HEAD_DIM, outs[0], outs[1]).astype(o_ref.dtype)


def _sb_attention(u_sb, width):
    b, s, _ = u_sb.shape
    pairs = width // LANES
    kernel = functools.partial(_sb_attn_kernel, scale=1.0 / math.sqrt(HEAD_DIM))
    return pl.pallas_call(
        kernel,
        grid=(b, pairs, s // Q_BLOCK),
        in_specs=[pl.BlockSpec((1, Q_BLOCK, LANES), lambda bi, hp, qi: (bi, qi, hp)),
                  pl.BlockSpec((1, s, LANES), lambda bi, hp, qi: (bi, 0, pairs + hp)),
                  pl.BlockSpec((1, s, LANES), lambda bi, hp, qi: (bi, 0, 2 * pairs + hp))],
        out_specs=pl.BlockSpec((1, Q_BLOCK, LANES), lambda bi, hp, qi: (bi, qi, hp)),
        out_shape=jax.ShapeDtypeStruct((b, s, width), BF16),
        compiler_params=pltpu.CompilerParams(
            dimension_semantics=("parallel", "parallel", "arbitrary"),
            vmem_limit_bytes=VMEM_LIMIT),
        name="sb_attention",
    )(u_sb, u_sb, u_sb)


def _rwkv_kernel(rkv_ref, lora_ref, mu_rkv_ref, mu_lora_ref, w0_ref, w2_ref, a0_ref, a2_ref,
                 g2_ref, kk_ref, ka_ref, rk_ref, lnw_ref, lnb_ref, o_ref,
                 state_ref, prev_rkv_ref, prev_lora_ref):
    c_len = rkv_ref.shape[1]
    width = o_ref.shape[2]
    pairs = width // LANES

    @pl.when(pl.program_id(1) == 0)
    def _():
        state_ref[...] = jnp.zeros_like(state_ref)
        prev_rkv_ref[...] = jnp.zeros_like(prev_rkv_ref)
        prev_lora_ref[...] = jnp.zeros_like(prev_lora_ref)

    def token_shift(u_ref, prev_ref, mu_ref):
        u = u_ref[0]
        first = lax.broadcasted_iota(jnp.int32, u.shape, 0) == 0
        prev = jnp.where(first, prev_ref[...], pltpu.roll(u, 1, 0))
        prev_ref[...] = u[c_len - 1:c_len, :]
        return u + (prev - u) * mu_ref[...]

    rkv = token_shift(rkv_ref, prev_rkv_ref, mu_rkv_ref)
    lora = token_shift(lora_ref, prev_lora_ref, mu_lora_ref)
    r = rkv[:, :width]
    k = rkv[:, width:2 * width]
    v = rkv[:, 2 * width:]
    x_wa = lora[:, :DECAY_LORA + ICLR_LORA]
    x_g = lora[:, DECAY_LORA + ICLR_LORA:]

    d_w = w0_ref[...] + _dot(jnp.tanh(x_wa).astype(BF16), w2_ref[...])
    log_decay = -jnp.exp(-_softplus(-d_w) - 0.5)
    iclr = jax.nn.sigmoid(a0_ref[...] + _dot(x_wa.astype(BF16), a2_ref[...]))
    gate = _dot(jax.nn.sigmoid(x_g).astype(BF16), g2_ref[...])

    lane_i = lax.broadcasted_iota(jnp.int32, (LANES, LANES), 0)
    lane_j = lax.broadcasted_iota(jnp.int32, (LANES, LANES), 1)
    same_head = lane_i // HEAD_DIM == lane_j // HEAD_DIM
    head_ones = same_head.astype(BF16)

    def head_sum(x):
        hi, lo = _split2(x)
        return _dot(hi, head_ones) + _dot(lo, head_ones)

    def head_sum_all(x):
        return jnp.concatenate(
            [head_sum(x[:, p * LANES:(p + 1) * LANES]) for p in range(pairs)], axis=1)

    kk = k * kk_ref[...]
    kk = kk / jnp.maximum(jnp.sqrt(head_sum_all(kk * kk)), 1e-12)
    k_eff = k * (1.0 + (iclr - 1.0) * ka_ref[...])

    t_i = lax.broadcasted_iota(jnp.int32, (c_len, c_len), 0)
    t_j = lax.broadcasted_iota(jnp.int32, (c_len, c_len), 1)
    tri = (t_j <= t_i).astype(BF16)
    cum = sum(_dot(tri, part) for part in _split3(log_decay))
    w_t = jnp.exp(cum)
    inv_w = jnp.exp(-cum)
    w_last = w_t[c_len - 1:c_len, :]
    r_t = r * w_t
    a_t = -kk * jnp.exp(cum - log_decay)
    b_t = kk * iclr * inv_w
    k_t = k_eff * inv_w
    b_end = b_t * w_last
    k_end = k_t * w_last
    bonus_rk = r * k_eff * rk_ref[...]

    row = lax.broadcasted_iota(jnp.int32, (c_len, LANES), 0)
    lane = lax.broadcasted_iota(jnp.int32, (c_len, LANES), 1)
    col = lane % HEAD_DIM
    strict = col < row
    incl = col <= row
    eye = (col == row).astype(F32)

    def bd(x):
        m = lax.broadcasted_iota(jnp.int32, x.shape, 1) < HEAD_DIM
        zero = jnp.zeros_like(x)
        return jnp.concatenate([jnp.where(m, x, zero), jnp.where(m, zero, x)], axis=0)

    for p in range(pairs):
        sl = slice(p * LANES, (p + 1) * LANES)
        lhs = jnp.concatenate([a_t[:, sl], r_t[:, sl]], axis=0).astype(BF16)
        rhs = jnp.concatenate([bd(b_t[:, sl]), bd(k_t[:, sl])], axis=0).astype(BF16)
        a_all = _dot_nt(lhs, rhs)
        a_ab = jnp.where(strict, a_all[:c_len, :LANES], 0.0)
        a_ak = jnp.where(strict, a_all[:c_len, LANES:], 0.0)
        a_rb = jnp.where(incl, a_all[c_len:, :LANES], 0.0)
        a_rk = jnp.where(incl, a_all[c_len:, LANES:], 0.0)

        power = a_ab
        t_inv = eye + a_ab
        n = 2
        while n < c_len:
            power = _dot(power.astype(BF16), bd(power).astype(BF16))
            t_inv = t_inv + _dot(t_inv.astype(BF16), bd(power).astype(BF16))
            n *= 2

        state = state_ref[p]
        v_p = v[:, sl]
        bd_v = bd(v_p).astype(BF16)
        from_state = _dot_nt(lhs, state.astype(BF16))
        u = _dot(t_inv.astype(BF16),
                 bd(from_state[:c_len] + _dot(a_ak.astype(BF16), bd_v)).astype(BF16))
        y = from_state[c_len:] + _dot(
            jnp.concatenate([a_rb, a_rk], axis=1).astype(BF16),
            jnp.concatenate([bd(u).astype(BF16), bd_v], axis=0))
        uv = jnp.concatenate([u, v_p], axis=0)
        bk = jnp.concatenate([b_end[:, sl], k_end[:, sl]], axis=0)
        grown = _dot(uv.T.astype(BF16), bk.astype(BF16))
        state_ref[p] = state * w_last[:, sl] + jnp.where(same_head, grown, 0.0)

        mean = head_sum(y) * (1.0 / HEAD_DIM)
        d = y - mean
        var = head_sum(d * d) * (1.0 / HEAD_DIM)
        y = d * lax.rsqrt(var + GN_EPS) * lnw_ref[:, sl] + lnb_ref[:, sl]
        y = y + head_sum(bonus_rk[:, sl]) * v_p
        o_ref[0, :, sl] = (y * gate[:, sl]).astype(o_ref.dtype)


def _rwkv(u_rkv, u_lora, params, width):
    b, s, _ = u_rkv.shape
    c_len = RW_CHUNK
    chunk = lambda n: pl.BlockSpec((1, c_len, n), lambda bi, ci: (bi, ci, 0))
    return pl.pallas_call(
        _rwkv_kernel,
        grid=(b, s // c_len),
        in_specs=[chunk(u_rkv.shape[2]), chunk(u_lora.shape[2])] + [_resident(a.shape) for a in params],
        out_specs=chunk(width),
        out_shape=jax.ShapeDtypeStruct((b, s, width), BF16),
        scratch_shapes=[pltpu.VMEM((width // LANES, LANES, LANES), F32),
                        pltpu.VMEM((1, u_rkv.shape[2]), F32),
                        pltpu.VMEM((1, u_lora.shape[2]), F32)],
        compiler_params=pltpu.CompilerParams(dimension_semantics=("parallel", "arbitrary"),
                                             vmem_limit_bytes=VMEM_LIMIT),
        name="rwkv7",
    )(u_rkv, u_lora, *params)


def _post_kernel(x_ref, osb_ref, orw_ref, gate_ref, p_ref, wupsb_ref, wuprw_ref, wout_ref,
                 gmlp_ref, wff1_ref, wff2_ref, gple_ref, wpg_ref, wpp_ref, gfin_ref, o_ref,
                 *, ff_chunk, final):
    x = x_ref[...]
    d = x.shape[1]
    merged = (gate_ref[:, :d].astype(F32) * _dot(osb_ref[...], wupsb_ref[...])
              + gate_ref[:, d:].astype(F32) * _dot(orw_ref[...], wuprw_ref[...]))
    x = x + _dot(merged.astype(BF16), wout_ref[...])

    h = _rms(x, gmlp_ref[...]).astype(BF16)
    mlp = jnp.zeros_like(x)
    for f0 in range(0, wff1_ref.shape[1], ff_chunk):
        hid = jnp.maximum(_dot(h, wff1_ref[:, f0:f0 + ff_chunk]), 0.0)
        mlp = mlp + _dot((hid * hid).astype(BF16), wff2_ref[f0:f0 + ff_chunk, :])
    x = x + mlp

    g = jax.nn.sigmoid(_dot(_rms(x, gple_ref[...]).astype(BF16), wpg_ref[...]))
    x = x + g * _dot(p_ref[...].astype(BF16), wpp_ref[...])
    o_ref[...] = _rms(x, gfin_ref[...]) if final else x


def _post(x2, o_sb, o_rw, gates, p2, weights, tm, final):
    t, d = x2.shape
    row = lambda n: pl.BlockSpec((tm, n), lambda i: (i, 0))
    kernel = functools.partial(_post_kernel, ff_chunk=1024, final=final)
    return pl.pallas_call(
        kernel,
        grid=(t // tm,),
        in_specs=[row(d), row(o_sb.shape[1]), row(o_rw.shape[1]), row(gates.shape[1]), row(p2.shape[1])]
                 + [_resident(w.shape) for w in weights],
        out_specs=row(d),
        out_shape=jax.ShapeDtypeStruct((t, d), F32),
        compiler_params=pltpu.CompilerParams(dimension_semantics=("parallel",),
                                             vmem_limit_bytes=VMEM_LIMIT),
        name="post",
    )(x2, o_sb, o_rw, gates, p2, *weights)


def kernel(x, p, attn_norm_g, w_in, shift_mu, decay_w0, decay_w2, iclr_a0, iclr_a2, gate_g2, k_k, k_a, r_k, ln_x_w, ln_x_b, w_up_sb, w_up_rw, w_out, mlp_norm_g, w_ff1, w_ff2, ple_norm_g, w_ple_gate, w_ple_proj, final_norm_g):
    bsz, seq, d = x.shape
    depth = w_in.shape[0]
    sb_width = w_up_sb.shape[1]
    rw_width = w_up_rw.shape[1]
    sb_cols = 3 * sb_width
    rkv_cols = 3 * rw_width
    lora_cols = DECAY_LORA + ICLR_LORA + GATE_LORA
    rw_end = sb_cols + rkv_cols + lora_cols
    tokens = bsz * seq
    tm_in = min(512, tokens)
    tm_post = min(256, tokens)
    vec = lambda a: a.reshape(1, -1).astype(F32)
    bf = lambda a: a.astype(BF16)

    x2 = x.reshape(tokens, d)
    for i in range(depth):
        wi = w_in[i]
        u_sb, u_rkv, u_lora, gates = _inproj(
            x2, vec(attn_norm_g[i]), bf(wi[:, :sb_cols]), bf(wi[:, sb_cols:sb_cols + rkv_cols]),
            bf(wi[:, sb_cols + rkv_cols:rw_end]), bf(wi[:, rw_end:]), tm_in)

        o_sb = _sb_attention(u_sb.reshape(bsz, seq, sb_cols), sb_width)

        mu = shift_mu[i]
        zeros = jnp.zeros((DECAY_LORA, rw_width), F32)
        rw_params = (
            vec(mu[:rkv_cols]), vec(mu[rkv_cols:]),
            vec(decay_w0[i]), bf(jnp.concatenate([decay_w2[i], zeros], axis=0)),
            vec(iclr_a0[i]), bf(jnp.concatenate([zeros, iclr_a2[i]], axis=0)),
            bf(gate_g2[i]), vec(k_k[i]), vec(k_a[i]), vec(r_k[i]), vec(ln_x_w[i]), vec(ln_x_b[i]))
        o_rw = _rwkv(u_rkv.reshape(bsz, seq, rkv_cols), u_lora.reshape(bsz, seq, lora_cols),
                     rw_params, rw_width)

        post_weights = (bf(w_up_sb[i]), bf(w_up_rw[i]), bf(w_out[i]), vec(mlp_norm_g[i]),
                        bf(w_ff1[i]), bf(w_ff2[i]), vec(ple_norm_g[i]), bf(w_ple_gate[i]),
                        bf(w_ple_proj[i]), vec(final_norm_g))
        x2 = _post(x2, o_sb.reshape(tokens, sb_width), o_rw.reshape(tokens, rw_width), gates,
                   p[i].reshape(tokens, -1), post_weights, tm_post, final=(i == depth - 1))
    return x2.reshape(bsz, seq, d)
```

```python
import functools
import math

import jax
import jax.numpy as jnp
from jax import lax
from jax.experimental import pallas as pl
from jax.experimental.pallas import tpu as pltpu

NORM_EPS = 1e-6
GN_EPS = 64e-5
HEAD_DIM = 64
LANES = 128
SB_TILE = 256
RW_SEQS = 2
RW_CHUNK = 64
DECAY_LORA = 64
ICLR_LORA = 64
GATE_LORA = 128
VMEM_LIMIT = 56 * 1024 * 1024

F32 = jnp.float32
BF16 = jnp.bfloat16

_NT = (((1,), (1,)), ((), ()))


def _dot(a, b):
    return jnp.dot(a, b, preferred_element_type=F32)


def _dot_nt(a, b):
    return lax.dot_general(a, b, _NT, preferred_element_type=F32)


def _bdot(a, b):
    return lax.dot_general(a, b, (((2,), (1,)), ((0,), (0,))), preferred_element_type=F32)


def _bdot_nt(a, b):
    return lax.dot_general(a, b, (((2,), (2,)), ((0,), (0,))), preferred_element_type=F32)


def _split2(x):
    hi = x.astype(BF16)
    lo = (x - hi.astype(F32)).astype(BF16)
    return hi, lo


def _split3(x):
    hi = x.astype(BF16)
    r1 = x - hi.astype(F32)
    mid = r1.astype(BF16)
    lo = (r1 - mid.astype(F32)).astype(BF16)
    return hi, mid, lo


def _rms(x, g):
    return x * lax.rsqrt(jnp.mean(x * x, axis=-1, keepdims=True) + NORM_EPS) * g


def _softplus(x):
    return jnp.maximum(x, 0.0) + jnp.log1p(jnp.exp(-jnp.abs(x)))


def _resident(shape):
    nd = len(shape)
    return pl.BlockSpec(shape, lambda *_: (0,) * nd, pipeline_mode=pl.Buffered(1))


def _inproj_kernel(x_ref, g_ref, wsb_ref, wrkv_ref, wlora_ref, wgate_ref,
                   sb_ref, rkv_ref, lora_ref, gate_ref):
    hb = _rms(x_ref[...], g_ref[...]).astype(BF16)
    sb_ref[...] = _dot(hb, wsb_ref[...]).astype(sb_ref.dtype)
    rkv_ref[...] = _dot(hb, wrkv_ref[...])
    lora_ref[...] = _dot(hb, wlora_ref[...])
    gate_ref[...] = jax.nn.sigmoid(_dot(hb, wgate_ref[...])).astype(gate_ref.dtype)


def _inproj(x2, g, wsb, wrkv, wlora, wgate, tm):
    t, d = x2.shape
    row = lambda n: pl.BlockSpec((tm, n), lambda i: (i, 0))
    ws = (wsb, wrkv, wlora, wgate)
    return pl.pallas_call(
        _inproj_kernel,
        grid=(t // tm,),
        in_specs=[row(d), _resident(g.shape)] + [_resident(w.shape) for w in ws],
        out_specs=[row(w.shape[1]) for w in ws],
        out_shape=[jax.ShapeDtypeStruct((t, wsb.shape[1]), BF16),
                   jax.ShapeDtypeStruct((t, wrkv.shape[1]), F32),
                   jax.ShapeDtypeStruct((t, wlora.shape[1]), F32),
                   jax.ShapeDtypeStruct((t, wgate.shape[1]), BF16)],
        compiler_params=pltpu.CompilerParams(dimension_semantics=("parallel",),
                                             vmem_limit_bytes=VMEM_LIMIT),
        name="inproj",
    )(x2, g, *ws)


def _sb_attn_kernel(q_ref, k_ref, v_ref, o_ref, *, scale):
    qi = pl.program_id(2)
    tile = q_ref.shape[1]
    heads = LANES // HEAD_DIM
    q = q_ref[0] * scale
    lane = lax.broadcasted_iota(jnp.int32, (tile, LANES), 1)
    zero = jnp.zeros_like(q)
    qh = jnp.concatenate([jnp.where(lane // HEAD_DIM == h, q, zero) for h in range(heads)], axis=0)
    row = lax.broadcasted_iota(jnp.int32, (heads * tile, tile), 0) % tile
    col = lax.broadcasted_iota(jnp.int32, (heads * tile, tile), 1)
    strict = col < row
    later = (lax.broadcasted_iota(jnp.int32, (tile, tile), 0)
             > lax.broadcasted_iota(jnp.int32, (tile, tile), 1)).astype(BF16)

    def key_tile(j, c, acc, on_diagonal):
        start = pl.multiple_of(j * tile, tile)
        kb = k_ref[0, pl.ds(start, tile), :]
        vb = v_ref[0, pl.ds(start, tile), :]
        z = _dot_nt(qh, kb)
        t = jnp.log1p(jnp.exp(-jnp.abs(z)))
        log_beta = jnp.minimum(z, 0.0) - t
        log_om = jnp.minimum(-z, 0.0) - t
        if on_diagonal:
            log_om = jnp.where(strict, log_om, 0.0)
        hi, lo = _split2(log_om)
        after = _dot(hi, later) + _dot(lo, later)
        w = jnp.exp(log_beta + after + c)
        if on_diagonal:
            w = jnp.where(strict, w, 0.0)
        acc = acc + _dot(w.astype(BF16), vb)
        c = c + after[:, 0:1] + log_om[:, 0:1]
        return c, acc

    c = jnp.zeros((heads * tile, 1), F32)
    acc = jnp.zeros((heads * tile, LANES), F32)
    c, acc = key_tile(qi, c, acc, True)
    c, acc = lax.fori_loop(0, qi, lambda i, ca: key_tile(qi - 1 - i, ca[0], ca[1], False), (c, acc))
    out = acc[:tile]
    for h in range(1, heads):
        out = jnp.where(lane // HEAD_DIM == h, acc[h * tile:(h + 1) * tile], out)
    o_ref[0] = out.astype(o_ref.dtype)


def _sb_attention(u_sb, width, tile):
    b, s, _ = u_sb.shape
    pairs = width // LANES
    kernel = functools.partial(_sb_attn_kernel, scale=1.0 / math.sqrt(HEAD_DIM))
    return pl.pallas_call(
        kernel,
        grid=(b, pairs, s // tile),
        in_specs=[pl.BlockSpec((1, tile, LANES), lambda bi, hp, qi: (bi, qi, hp)),
                  pl.BlockSpec((1, s, LANES), lambda bi, hp, qi: (bi, 0, pairs + hp)),
                  pl.BlockSpec((1, s, LANES), lambda bi, hp, qi: (bi, 0, 2 * pairs + hp))],
        out_specs=pl.BlockSpec((1, tile, LANES), lambda bi, hp, qi: (bi, qi, hp)),
        out_shape=jax.ShapeDtypeStruct((b, s, width), BF16),
        compiler_params=pltpu.CompilerParams(
            dimension_semantics=("parallel", "parallel", "arbitrary"),
            vmem_limit_bytes=VMEM_LIMIT),
        name="sb_attention",
    )(u_sb, u_sb, u_sb)


def _rwkv_kernel(rkv_ref, lora_ref, mu_rkv_ref, mu_lora_ref, w0_ref, w2_ref, a0_ref, a2_ref,
                 g2_ref, kk_ref, ka_ref, rk_ref, lnw_ref, lnb_ref, o_ref,
                 state_ref, prev_rkv_ref, prev_lora_ref):
    nseq, c_len = rkv_ref.shape[0], rkv_ref.shape[1]
    width = o_ref.shape[2]
    pairs = width // LANES
    rows = nseq * c_len

    @pl.when(pl.program_id(1) == 0)
    def _():
        state_ref[...] = jnp.zeros_like(state_ref)
        prev_rkv_ref[...] = jnp.zeros_like(prev_rkv_ref)
        prev_lora_ref[...] = jnp.zeros_like(prev_lora_ref)

    def token_shift(u_ref, prev_ref, mu_ref):
        u = u_ref[...].reshape(rows, u_ref.shape[2])
        t = lax.broadcasted_iota(jnp.int32, u.shape, 0)
        prev = pltpu.roll(u, 1, 0)
        for s in range(nseq):
            prev = jnp.where(t == s * c_len, prev_ref[s], prev)
            prev_ref[s] = u[(s + 1) * c_len - 1:(s + 1) * c_len, :]
        return u + (prev - u) * mu_ref[...]

    rkv = token_shift(rkv_ref, prev_rkv_ref, mu_rkv_ref)
    lora = token_shift(lora_ref, prev_lora_ref, mu_lora_ref)
    r = rkv[:, :width]
    k = rkv[:, width:2 * width]
    v = rkv[:, 2 * width:]
    x_wa = lora[:, :DECAY_LORA + ICLR_LORA]
    x_g = lora[:, DECAY_LORA + ICLR_LORA:]

    d_w = w0_ref[...] + _dot(jnp.tanh(x_wa).astype(BF16), w2_ref[...])
    log_decay = -jnp.exp(-_softplus(-d_w) - 0.5)
    iclr = jax.nn.sigmoid(a0_ref[...] + _dot(x_wa.astype(BF16), a2_ref[...]))
    gate = _dot(jax.nn.sigmoid(x_g).astype(BF16), g2_ref[...])

    lane_i = lax.broadcasted_iota(jnp.int32, (LANES, LANES), 0)
    lane_j = lax.broadcasted_iota(jnp.int32, (LANES, LANES), 1)
    same_head = lane_i // HEAD_DIM == lane_j // HEAD_DIM
    head_ones = same_head.astype(BF16)

    def head_sum(x):
        hi, lo = _split2(x.reshape(-1, LANES))
        return (_dot(hi, head_ones) + _dot(lo, head_ones)).reshape(x.shape)

    def group(x):
        return jnp.stack([x[s * c_len:(s + 1) * c_len, p * LANES:(p + 1) * LANES]
                          for s in range(nseq) for p in range(pairs)])

    def group_param(ref):
        return jnp.stack([ref[:, p * LANES:(p + 1) * LANES] for _ in range(nseq) for p in range(pairs)])

    k_eff = group(k * (1.0 + (iclr - 1.0) * ka_ref[...]))
    kk = group(k * kk_ref[...])
    kk = kk / jnp.maximum(jnp.sqrt(head_sum(kk * kk)), 1e-12)

    t_i = lax.broadcasted_iota(jnp.int32, (rows, rows), 0)
    t_j = lax.broadcasted_iota(jnp.int32, (rows, rows), 1)
    tri = ((t_j <= t_i) & (t_i // c_len == t_j // c_len)).astype(BF16)
    cum_flat = sum(_dot(tri, part) for part in _split3(log_decay))
    cum = group(cum_flat)
    w_prev = group(jnp.exp(cum_flat - log_decay))
    w_t = jnp.exp(cum)
    inv_w = jnp.exp(-cum)
    w_last = w_t[:, c_len - 1:c_len, :]
    r = group(r)
    v = group(v)
    iclr = group(iclr)
    r_t = r * w_t
    a_t = -kk * w_prev
    b_t = kk * iclr * inv_w
    k_t = k_eff * inv_w

    row = lax.broadcasted_iota(jnp.int32, (c_len, LANES), 0)
    col = lax.broadcasted_iota(jnp.int32, (c_len, LANES), 1) % HEAD_DIM
    strict = col < row
    incl = col <= row
    eye = (col == row).astype(F32)

    def bd(x):
        m = lax.broadcasted_iota(jnp.int32, x.shape, 2) < HEAD_DIM
        zero = jnp.zeros_like(x)
        return jnp.concatenate([jnp.where(m, x, zero), jnp.where(m, zero, x)], axis=1)

    lhs = jnp.concatenate([a_t, r_t], axis=1).astype(BF16)
    rhs = jnp.concatenate([bd(b_t), bd(k_t)], axis=1).astype(BF16)
    a_all = _bdot_nt(lhs, rhs)
    a_ab = jnp.where(strict, a_all[:, :c_len, :LANES], 0.0)
    a_ak = jnp.where(strict, a_all[:, :c_len, LANES:], 0.0)
    a_rb = jnp.where(incl, a_all[:, c_len:, :LANES], 0.0)
    a_rk = jnp.where(incl, a_all[:, c_len:, LANES:], 0.0)

    power = a_ab
    t_inv = eye + a_ab
    n = 2
    while n < c_len:
        power = _bdot(power.astype(BF16), bd(power).astype(BF16))
        t_inv = t_inv + _bdot(t_inv.astype(BF16), bd(power).astype(BF16))
        n *= 2

    state = state_ref[...]
    bd_v = bd(v).astype(BF16)
    from_state = _bdot_nt(lhs, state.astype(BF16))
    u = _bdot(t_inv.astype(BF16),
              bd(from_state[:, :c_len] + _bdot(a_ak.astype(BF16), bd_v)).astype(BF16))
    y = from_state[:, c_len:] + _bdot(
        jnp.concatenate([a_rb, a_rk], axis=2).astype(BF16),
        jnp.concatenate([bd(u).astype(BF16), bd_v], axis=1))
    uv = jnp.concatenate([u, v], axis=1)
    bk = jnp.concatenate([b_t * w_last, k_t * w_last], axis=1)
    grown = _bdot(jnp.swapaxes(uv, 1, 2).astype(BF16), bk.astype(BF16))
    state_ref[...] = state * w_last + jnp.where(same_head, grown, 0.0)

    mean = head_sum(y) * (1.0 / HEAD_DIM)
    d = y - mean
    var = head_sum(d * d) * (1.0 / HEAD_DIM)
    y = d * lax.rsqrt(var + GN_EPS) * group_param(lnw_ref) + group_param(lnb_ref)
    y = y + head_sum(r * k_eff * group_param(rk_ref)) * v
    out = (y * group(gate)).astype(o_ref.dtype)
    for s in range(nseq):
        for p in range(pairs):
            o_ref[s, :, p * LANES:(p + 1) * LANES] = out[s * pairs + p]


def _rwkv(u_rkv, u_lora, params, width, nseq):
    b, s, _ = u_rkv.shape
    c_len = RW_CHUNK
    chunk = lambda n: pl.BlockSpec((nseq, c_len, n), lambda bi, ci: (bi, ci, 0))
    return pl.pallas_call(
        _rwkv_kernel,
        grid=(b // nseq, s // c_len),
        in_specs=[chunk(u_rkv.shape[2]), chunk(u_lora.shape[2])] + [_resident(a.shape) for a in params],
        out_specs=chunk(width),
        out_shape=jax.ShapeDtypeStruct((b, s, width), BF16),
        scratch_shapes=[pltpu.VMEM((nseq * (width // LANES), LANES, LANES), F32),
                        pltpu.VMEM((nseq, 1, u_rkv.shape[2]), F32),
                        pltpu.VMEM((nseq, 1, u_lora.shape[2]), F32)],
        compiler_params=pltpu.CompilerParams(dimension_semantics=("parallel", "arbitrary"),
                                             vmem_limit_bytes=VMEM_LIMIT),
        name="rwkv7",
    )(u_rkv, u_lora, *params)


def _post_kernel(x_ref, osb_ref, orw_ref, gate_ref, p_ref, wupsb_ref, wuprw_ref, wout_ref,
                 gmlp_ref, wff1_ref, wff2_ref, gple_ref, wpg_ref, wpp_ref, gfin_ref, o_ref,
                 *, ff_chunk, final):
    x = x_ref[...]
    d = x.shape[1]
    merged = (gate_ref[:, :d].astype(F32) * _dot(osb_ref[...], wupsb_ref[...])
              + gate_ref[:, d:].astype(F32) * _dot(orw_ref[...], wuprw_ref[...]))
    x = x + _dot(merged.astype(BF16), wout_ref[...])

    h = _rms(x, gmlp_ref[...]).astype(BF16)
    mlp = jnp.zeros_like(x)
    for f0 in range(0, wff1_ref.shape[1], ff_chunk):
        hid = jnp.maximum(_dot(h, wff1_ref[:, f0:f0 + ff_chunk]), 0.0)
        mlp = mlp + _dot((hid * hid).astype(BF16), wff2_ref[f0:f0 + ff_chunk, :])
    x = x + mlp

    g = jax.nn.sigmoid(_dot(_rms(x, gple_ref[...]).astype(BF16), wpg_ref[...]))
    x = x + g * _dot(p_ref[...].astype(BF16), wpp_ref[...])
    o_ref[...] = _rms(x, gfin_ref[...]) if final else x


def _post(x2, o_sb, o_rw, gates, p2, weights, tm, final):
    t, d = x2.shape
    row = lambda n: pl.BlockSpec((tm, n), lambda i: (i, 0))
    kernel = functools.partial(_post_kernel, ff_chunk=1024, final=final)
    return pl.pallas_call(
        kernel,
        grid=(t // tm,),
        in_specs=[row(d), row(o_sb.shape[1]), row(o_rw.shape[1]), row(gates.shape[1]), row(p2.shape[1])]
                 + [_resident(w.shape) for w in weights],
        out_specs=row(d),
        out_shape=jax.ShapeDtypeStruct((t, d), F32),
        compiler_params=pltpu.CompilerParams(dimension_semantics=("parallel",),
                                             vmem_limit_bytes=VMEM_LIMIT),
        name="post",
    )(x2, o_sb, o_rw, gates, p2, *weights)


def kernel(x, p, attn_norm_g, w_in, shift_mu, decay_w0, decay_w2, iclr_a0, iclr_a2, gate_g2, k_k, k_a, r_k, ln_x_w, ln_x_b, w_up_sb, w_up_rw, w_out, mlp_norm_g, w_ff1, w_ff2, ple_norm_g, w_ple_gate, w_ple_proj, final_norm_g):
    bsz, seq, d = x.shape
    depth = w_in.shape[0]
    sb_width = w_up_sb.shape[1]
    rw_width = w_up_rw.shape[1]
    sb_cols = 3 * sb_width
    rkv_cols = 3 * rw_width
    lora_cols = DECAY_LORA + ICLR_LORA + GATE_LORA
    rw_end = sb_cols + rkv_cols + lora_cols
    tokens = bsz * seq
    tm_in = min(512, tokens)
    tm_post = min(256, tokens)
    vec = lambda a: a.reshape(1, -1).astype(F32)
    bf = lambda a: a.astype(BF16)

    x2 = x.reshape(tokens, d)
    for i in range(depth):
        wi = w_in[i]
        u_sb, u_rkv, u_lora, gates = _inproj(
            x2, vec(attn_norm_g[i]), bf(wi[:, :sb_cols]), bf(wi[:, sb_cols:sb_cols + rkv_cols]),
            bf(wi[:, sb_cols + rkv_cols:rw_end]), bf(wi[:, rw_end:]), tm_in)

        o_sb = _sb_attention(u_sb.reshape(bsz, seq, sb_cols), sb_width, min(SB_TILE, seq))

        mu = shift_mu[i]
        w2_pad = jnp.concatenate([decay_w2[i], jnp.zeros((ICLR_LORA, rw_width), F32)], axis=0)
        a2_pad = jnp.concatenate([jnp.zeros((DECAY_LORA, rw_width), F32), iclr_a2[i]], axis=0)
        rw_params = (
            vec(mu[:rkv_cols]), vec(mu[rkv_cols:]), vec(decay_w0[i]), bf(w2_pad), vec(iclr_a0[i]),
            bf(a2_pad), bf(gate_g2[i]), vec(k_k[i]), vec(k_a[i]), vec(r_k[i]), vec(ln_x_w[i]),
            vec(ln_x_b[i]))
        o_rw = _rwkv(u_rkv.reshape(bsz, seq, rkv_cols), u_lora.reshape(bsz, seq, lora_cols),
                     rw_params, rw_width, RW_SEQS if bsz % RW_SEQS == 0 else 1)

        post_weights = (bf(w_up_sb[i]), bf(w_up_rw[i]), bf(w_out[i]), vec(mlp_norm_g[i]),
                        bf(w_ff1[i]), bf(w_ff2[i]), vec(ple_norm_g[i]), bf(w_ple_gate[i]),
                        bf(w_ple_proj[i]), vec(final_norm_g))
        x2 = _post(x2, o_sb.reshape(tokens, sb_width), o_rw.reshape(tokens, rw_width), gates,
                   p[i].reshape(tokens, -1), post_weights, tm_post, final=(i == depth - 1))
    return x2.reshape(bsz, seq, d)
```

```python
import functools
import math

import jax
import jax.numpy as jnp
from jax import lax
from jax.experimental import pallas as pl
from jax.experimental.pallas import tpu as pltpu

NORM_EPS = 1e-6
GN_EPS = 64e-5
HEAD_DIM = 64
LANES = 128
SB_TILE = 256
RW_SEQS = 4
RW_CHUNK = 64
DECAY_LORA = 64
ICLR_LORA = 64
GATE_LORA = 128
VMEM_LIMIT = 56 * 1024 * 1024

F32 = jnp.float32
BF16 = jnp.bfloat16

_NT = (((1,), (1,)), ((), ()))


def _dot(a, b):
    return jnp.dot(a, b, preferred_element_type=F32)


def _dot_nt(a, b):
    return lax.dot_general(a, b, _NT, preferred_element_type=F32)


def _bdot(a, b):
    return lax.dot_general(a, b, (((2,), (1,)), ((0,), (0,))), preferred_element_type=F32)


def _bdot_nt(a, b):
    return lax.dot_general(a, b, (((2,), (2,)), ((0,), (0,))), preferred_element_type=F32)


def _split2(x):
    hi = x.astype(BF16)
    lo = (x - hi.astype(F32)).astype(BF16)
    return hi, lo


def _split3(x):
    hi = x.astype(BF16)
    r1 = x - hi.astype(F32)
    mid = r1.astype(BF16)
    lo = (r1 - mid.astype(F32)).astype(BF16)
    return hi, mid, lo


def _rms(x, g):
    return x * lax.rsqrt(jnp.mean(x * x, axis=-1, keepdims=True) + NORM_EPS) * g


def _softplus(x):
    return jnp.maximum(x, 0.0) + jnp.log1p(jnp.exp(-jnp.abs(x)))


def _resident(shape):
    nd = len(shape)
    return pl.BlockSpec(shape, lambda *_: (0,) * nd, pipeline_mode=pl.Buffered(1))


def _inproj_kernel(x_ref, g_ref, wsb_ref, wrkv_ref, wlora_ref, wgate_ref,
                   sb_ref, rkv_ref, lora_ref, gate_ref):
    hb = _rms(x_ref[...], g_ref[...]).astype(BF16)
    sb_ref[...] = _dot(hb, wsb_ref[...]).astype(sb_ref.dtype)
    rkv_ref[...] = _dot(hb, wrkv_ref[...])
    lora_ref[...] = _dot(hb, wlora_ref[...])
    gate_ref[...] = jax.nn.sigmoid(_dot(hb, wgate_ref[...])).astype(gate_ref.dtype)


def _inproj(x2, g, wsb, wrkv, wlora, wgate, tm):
    t, d = x2.shape
    row = lambda n: pl.BlockSpec((tm, n), lambda i: (i, 0))
    ws = (wsb, wrkv, wlora, wgate)
    return pl.pallas_call(
        _inproj_kernel,
        grid=(t // tm,),
        in_specs=[row(d), _resident(g.shape)] + [_resident(w.shape) for w in ws],
        out_specs=[row(w.shape[1]) for w in ws],
        out_shape=[jax.ShapeDtypeStruct((t, wsb.shape[1]), BF16),
                   jax.ShapeDtypeStruct((t, wrkv.shape[1]), F32),
                   jax.ShapeDtypeStruct((t, wlora.shape[1]), F32),
                   jax.ShapeDtypeStruct((t, wgate.shape[1]), BF16)],
        compiler_params=pltpu.CompilerParams(dimension_semantics=("parallel",),
                                             vmem_limit_bytes=VMEM_LIMIT),
        name="inproj",
    )(x2, g, *ws)


LOG2_E = math.log2(math.e)
MASKED = -1e30


def _sb_attn_kernel(q_ref, k_ref, v_ref, o_ref, hl_ref, z_ref, bias_ref, *, scale, tile):
    heads = LANES // HEAD_DIM
    rows = heads * tile
    n_tiles = q_ref.shape[1] // tile
    lane = lax.broadcasted_iota(jnp.int32, (tile, LANES), 1)
    row = lax.broadcasted_iota(jnp.int32, (rows, tile), 0) % tile
    col = lax.broadcasted_iota(jnp.int32, (rows, tile), 1)
    bias_ref[0] = jnp.zeros((rows, tile), F32)
    bias_ref[1] = jnp.where(col < row, 0.0, MASKED)
    key_j = lax.broadcasted_iota(jnp.int32, (2 * tile, tile), 0) % tile
    key_s = lax.broadcasted_iota(jnp.int32, (2 * tile, tile), 1)
    from_key = (key_j >= key_s).astype(BF16)

    def score_matmul(qt, kt):
        q = q_ref[0, pl.ds(pl.multiple_of(qt * tile, tile), tile), :] * scale
        zero = jnp.zeros_like(q)
        qh = jnp.concatenate([jnp.where(lane // HEAD_DIM == h, q, zero) for h in range(heads)], axis=0)
        kb = k_ref[0, pl.ds(pl.multiple_of(kt * tile, tile), tile), :]
        return _dot_nt(qh, kb)

    def split_scores(z, qt, kt, slot):
        z = z + bias_ref[(qt == kt).astype(jnp.int32)]
        log_om = (jnp.minimum(z, 0.0) - z) - jnp.log(1.0 + jnp.exp2(jnp.abs(z) * -LOG2_E))
        hi, lo = _split2(log_om)
        hl_ref[slot, :, :tile] = hi
        hl_ref[slot, :, tile:] = lo
        z_ref[slot] = z

    def weights(after, qt, kt, slot, c, acc):
        first = qt == kt
        c = jnp.where(first, 0.0, c)
        acc = jnp.where(first, 0.0, acc)
        w = jnp.exp(after + z_ref[slot] + c)
        vb = v_ref[0, pl.ds(pl.multiple_of(kt * tile, tile), tile), :]
        acc = acc + _dot(w.astype(BF16), vb)
        c = c + after[:, 0:1]
        out = acc[:tile]
        for h in range(1, heads):
            out = jnp.where(lane // HEAD_DIM == h, acc[h * tile:(h + 1) * tile], out)
        o_ref[0, pl.ds(pl.multiple_of(qt * tile, tile), tile), :] = out.astype(o_ref.dtype)
        return c, acc

    def pair(carry, slot):
        qt, kt, c, acc = carry
        last_key = kt == 0
        qt_next = jnp.where(last_key, qt + 1, qt)
        kt_next = jnp.where(last_key, qt + 1, kt - 1)
        qt_n = jnp.minimum(qt_next, n_tiles - 1)
        kt_n = jnp.minimum(kt_next, n_tiles - 1)
        z_next = score_matmul(qt_n, kt_n)
        after = _dot(hl_ref[slot], from_key)
        split_scores(z_next, qt_n, kt_n, 1 - slot)
        c, acc = weights(after, qt, kt, slot, c, acc)
        return qt_next, kt_next, c, acc

    n_pairs = n_tiles * (n_tiles + 1) // 2
    zero_i = jnp.zeros((), jnp.int32)
    split_scores(score_matmul(zero_i, zero_i), zero_i, zero_i, 0)
    carry = (zero_i, zero_i, jnp.zeros((rows, 1), F32), jnp.zeros((rows, LANES), F32))
    carry = lax.fori_loop(0, n_pairs // 2, lambda _, ca: pair(pair(ca, 0), 1), carry)
    if n_pairs % 2:
        pair(carry, 0)


def _sb_attention(u_sb, width, tile):
    b, s, _ = u_sb.shape
    pairs = width // LANES
    rows = (LANES // HEAD_DIM) * tile
    kernel = functools.partial(_sb_attn_kernel, scale=1.0 / math.sqrt(HEAD_DIM), tile=tile)
    seq_block = lambda col0: pl.BlockSpec((1, s, LANES), lambda bi, hp: (bi, 0, col0 + hp))
    return pl.pallas_call(
        kernel,
        grid=(b, pairs),
        in_specs=[seq_block(0), seq_block(pairs), seq_block(2 * pairs)],
        out_specs=seq_block(0),
        out_shape=jax.ShapeDtypeStruct((b, s, width), BF16),
        scratch_shapes=[pltpu.VMEM((2, rows, 2 * tile), BF16),
                        pltpu.VMEM((2, rows, tile), F32),
                        pltpu.VMEM((2, rows, tile), F32)],
        compiler_params=pltpu.CompilerParams(dimension_semantics=("parallel", "parallel"),
                                             vmem_limit_bytes=VMEM_LIMIT),
        name="sb_attention",
    )(u_sb, u_sb, u_sb)


def _rwkv_kernel(rkv_ref, lora_ref, mu_rkv_ref, mu_lora_ref, w0_ref, w2_ref, a0_ref, a2_ref,
                 g2_ref, kk_ref, ka_ref, rk_ref, lnw_ref, lnb_ref, o_ref,
                 state_ref, prev_rkv_ref, prev_lora_ref):
    nseq, c_len = rkv_ref.shape[0], rkv_ref.shape[1]
    width = o_ref.shape[2]
    pairs = width // LANES
    rows = nseq * c_len

    @pl.when(pl.program_id(1) == 0)
    def _():
        state_ref[...] = jnp.zeros_like(state_ref)
        prev_rkv_ref[...] = jnp.zeros_like(prev_rkv_ref)
        prev_lora_ref[...] = jnp.zeros_like(prev_lora_ref)

    def token_shift(u_ref, prev_ref, mu_ref):
        u = u_ref[...].reshape(rows, u_ref.shape[2])
        t = lax.broadcasted_iota(jnp.int32, u.shape, 0)
        prev = pltpu.roll(u, 1, 0)
        for s in range(nseq):
            prev = jnp.where(t == s * c_len, prev_ref[s], prev)
            prev_ref[s] = u[(s + 1) * c_len - 1:(s + 1) * c_len, :]
        return u + (prev - u) * mu_ref[...]

    rkv = token_shift(rkv_ref, prev_rkv_ref, mu_rkv_ref)
    lora = token_shift(lora_ref, prev_lora_ref, mu_lora_ref)
    r = rkv[:, :width]
    k = rkv[:, width:2 * width]
    v = rkv[:, 2 * width:]
    x_wa = lora[:, :DECAY_LORA + ICLR_LORA]
    x_g = lora[:, DECAY_LORA + ICLR_LORA:]

    d_w = w0_ref[...] + _dot(jnp.tanh(x_wa).astype(BF16), w2_ref[...])
    log_decay = -jnp.exp(-_softplus(-d_w) - 0.5)
    iclr = jax.nn.sigmoid(a0_ref[...] + _dot(x_wa.astype(BF16), a2_ref[...]))
    gate = _dot(jax.nn.sigmoid(x_g).astype(BF16), g2_ref[...])

    lane_i = lax.broadcasted_iota(jnp.int32, (LANES, LANES), 0)
    lane_j = lax.broadcasted_iota(jnp.int32, (LANES, LANES), 1)
    same_head = lane_i // HEAD_DIM == lane_j // HEAD_DIM
    head_ones = same_head.astype(BF16)

    def head_sum(x):
        hi, lo = _split2(x.reshape(-1, LANES))
        return (_dot(hi, head_ones) + _dot(lo, head_ones)).reshape(x.shape)

    def group(x):
        return jnp.stack([x[s * c_len:(s + 1) * c_len, p * LANES:(p + 1) * LANES]
                          for s in range(nseq) for p in range(pairs)])

    def group_param(ref):
        return jnp.stack([ref[:, p * LANES:(p + 1) * LANES] for _ in range(nseq) for p in range(pairs)])

    k_eff = group(k * (1.0 + (iclr - 1.0) * ka_ref[...]))
    kk = group(k * kk_ref[...])
    kk = kk / jnp.maximum(jnp.sqrt(head_sum(kk * kk)), 1e-12)

    t_i = lax.broadcasted_iota(jnp.int32, (rows, rows), 0)
    t_j = lax.broadcasted_iota(jnp.int32, (rows, rows), 1)
    tri = ((t_j <= t_i) & (t_i // c_len == t_j // c_len)).astype(BF16)
    cum_flat = sum(_dot(tri, part) for part in _split3(log_decay))
    cum = group(cum_flat)
    w_prev = group(jnp.exp(cum_flat - log_decay))
    w_t = jnp.exp(cum)
    inv_w = jnp.exp(-cum)
    w_last = w_t[:, c_len - 1:c_len, :]
    r = group(r)
    v = group(v)
    iclr = group(iclr)
    r_t = r * w_t
    a_t = -kk * w_prev
    b_t = kk * iclr * inv_w
    k_t = k_eff * inv_w

    row = lax.broadcasted_iota(jnp.int32, (c_len, LANES), 0)
    col = lax.broadcasted_iota(jnp.int32, (c_len, LANES), 1) % HEAD_DIM
    strict = col < row
    incl = col <= row
    eye = (col == row).astype(F32)

    def bd(x):
        m = lax.broadcasted_iota(jnp.int32, x.shape, 2) < HEAD_DIM
        zero = jnp.zeros_like(x)
        return jnp.concatenate([jnp.where(m, x, zero), jnp.where(m, zero, x)], axis=1)

    lhs = jnp.concatenate([a_t, r_t], axis=1).astype(BF16)
    rhs = jnp.concatenate([bd(b_t), bd(k_t)], axis=1).astype(BF16)
    a_all = _bdot_nt(lhs, rhs)
    a_ab = jnp.where(strict, a_all[:, :c_len, :LANES], 0.0)
    a_ak = jnp.where(strict, a_all[:, :c_len, LANES:], 0.0)
    a_rb = jnp.where(incl, a_all[:, c_len:, :LANES], 0.0)
    a_rk = jnp.where(incl, a_all[:, c_len:, LANES:], 0.0)

    power = a_ab
    t_inv = eye + a_ab
    n = 2
    while n < c_len:
        power = _bdot(power.astype(BF16), bd(power).astype(BF16))
        t_inv = t_inv + _bdot(t_inv.astype(BF16), bd(power).astype(BF16))
        n *= 2

    state = state_ref[...]
    bd_v = bd(v).astype(BF16)
    from_state = _bdot_nt(lhs, state.astype(BF16))
    u = _bdot(t_inv.astype(BF16),
              bd(from_state[:, :c_len] + _bdot(a_ak.astype(BF16), bd_v)).astype(BF16))
    y = from_state[:, c_len:] + _bdot(
        jnp.concatenate([a_rb, a_rk], axis=2).astype(BF16),
        jnp.concatenate([bd(u).astype(BF16), bd_v], axis=1))
    uv = jnp.concatenate([u, v], axis=1)
    bk = jnp.concatenate([b_t * w_last, k_t * w_last], axis=1)
    grown = _bdot(jnp.swapaxes(uv, 1, 2).astype(BF16), bk.astype(BF16))
    state_ref[...] = state * w_last + jnp.where(same_head, grown, 0.0)

    mean = head_sum(y) * (1.0 / HEAD_DIM)
    d = y - mean
    var = head_sum(d * d) * (1.0 / HEAD_DIM)
    y = d * lax.rsqrt(var + GN_EPS) * group_param(lnw_ref) + group_param(lnb_ref)
    y = y + head_sum(r * k_eff * group_param(rk_ref)) * v
    out = (y * group(gate)).astype(o_ref.dtype)
    for s in range(nseq):
        for p in range(pairs):
            o_ref[s, :, p * LANES:(p + 1) * LANES] = out[s * pairs + p]


def _rwkv(u_rkv, u_lora, params, width, nseq):
    b, s, _ = u_rkv.shape
    c_len = RW_CHUNK
    chunk = lambda n: pl.BlockSpec((nseq, c_len, n), lambda bi, ci: (bi, ci, 0))
    return pl.pallas_call(
        _rwkv_kernel,
        grid=(b // nseq, s // c_len),
        in_specs=[chunk(u_rkv.shape[2]), chunk(u_lora.shape[2])] + [_resident(a.shape) for a in params],
        out_specs=chunk(width),
        out_shape=jax.ShapeDtypeStruct((b, s, width), BF16),
        scratch_shapes=[pltpu.VMEM((nseq * (width // LANES), LANES, LANES), F32),
                        pltpu.VMEM((nseq, 1, u_rkv.shape[2]), F32),
                        pltpu.VMEM((nseq, 1, u_lora.shape[2]), F32)],
        compiler_params=pltpu.CompilerParams(dimension_semantics=("parallel", "arbitrary"),
                                             vmem_limit_bytes=VMEM_LIMIT),
        name="rwkv7",
    )(u_rkv, u_lora, *params)


def _post_kernel(x_ref, osb_ref, orw_ref, gate_ref, p_ref, wupsb_ref, wuprw_ref, wout_ref,
                 gmlp_ref, wff1_ref, wff2_ref, gple_ref, wpg_ref, wpp_ref, gfin_ref, o_ref,
                 *, ff_chunk, final):
    x = x_ref[...]
    d = x.shape[1]
    merged = (gate_ref[:, :d].astype(F32) * _dot(osb_ref[...], wupsb_ref[...])
              + gate_ref[:, d:].astype(F32) * _dot(orw_ref[...], wuprw_ref[...]))
    x = x + _dot(merged.astype(BF16), wout_ref[...])

    h = _rms(x, gmlp_ref[...]).astype(BF16)
    mlp = jnp.zeros_like(x)
    for f0 in range(0, wff1_ref.shape[1], ff_chunk):
        hid = jnp.maximum(_dot(h, wff1_ref[:, f0:f0 + ff_chunk]), 0.0)
        mlp = mlp + _dot((hid * hid).astype(BF16), wff2_ref[f0:f0 + ff_chunk, :])
    x = x + mlp

    g = jax.nn.sigmoid(_dot(_rms(x, gple_ref[...]).astype(BF16), wpg_ref[...]))
    x = x + g * _dot(p_ref[...].astype(BF16), wpp_ref[...])
    o_ref[...] = _rms(x, gfin_ref[...]) if final else x


def _post(x2, o_sb, o_rw, gates, p2, weights, tm, final):
    t, d = x2.shape
    row = lambda n: pl.BlockSpec((tm, n), lambda i: (i, 0))
    kernel = functools.partial(_post_kernel, ff_chunk=1024, final=final)
    return pl.pallas_call(
        kernel,
        grid=(t // tm,),
        in_specs=[row(d), row(o_sb.shape[1]), row(o_rw.shape[1]), row(gates.shape[1]), row(p2.shape[1])]
                 + [_resident(w.shape) for w in weights],
        out_specs=row(d),
        out_shape=jax.ShapeDtypeStruct((t, d), F32),
        compiler_params=pltpu.CompilerParams(dimension_semantics=("parallel",),
                                             vmem_limit_bytes=VMEM_LIMIT),
        name="post",
    )(x2, o_sb, o_rw, gates, p2, *weights)


def kernel(x, p, attn_norm_g, w_in, shift_mu, decay_w0, decay_w2, iclr_a0, iclr_a2, gate_g2, k_k, k_a, r_k, ln_x_w, ln_x_b, w_up_sb, w_up_rw, w_out, mlp_norm_g, w_ff1, w_ff2, ple_norm_g, w_ple_gate, w_ple_proj, final_norm_g):
    bsz, seq, d = x.shape
    depth = w_in.shape[0]
    sb_width = w_up_sb.shape[1]
    rw_width = w_up_rw.shape[1]
    sb_cols = 3 * sb_width
    rkv_cols = 3 * rw_width
    lora_cols = DECAY_LORA + ICLR_LORA + GATE_LORA
    rw_end = sb_cols + rkv_cols + lora_cols
    tokens = bsz * seq
    tm_in = min(512, tokens)
    tm_post = min(512, tokens)
    vec = lambda a: a.reshape(1, -1).astype(F32)
    bf = lambda a: a.astype(BF16)

    x2 = x.reshape(tokens, d)
    for i in range(depth):
        wi = w_in[i]
        u_sb, u_rkv, u_lora, gates = _inproj(
            x2, vec(attn_norm_g[i]), bf(wi[:, :sb_cols]), bf(wi[:, sb_cols:sb_cols + rkv_cols]),
            bf(wi[:, sb_cols + rkv_cols:rw_end]), bf(wi[:, rw_end:]), tm_in)

        o_sb = _sb_attention(u_sb.reshape(bsz, seq, sb_cols), sb_width, min(SB_TILE, seq))

        mu = shift_mu[i]
        w2_pad = jnp.concatenate([decay_w2[i], jnp.zeros((ICLR_LORA, rw_width), F32)], axis=0)
        a2_pad = jnp.concatenate([jnp.zeros((DECAY_LORA, rw_width), F32), iclr_a2[i]], axis=0)
        rw_params = (
            vec(mu[:rkv_cols]), vec(mu[rkv_cols:]), vec(decay_w0[i]), bf(w2_pad), vec(iclr_a0[i]),
            bf(a2_pad), bf(gate_g2[i]), vec(k_k[i]), vec(k_a[i]), vec(r_k[i]), vec(ln_x_w[i]),
            vec(ln_x_b[i]))
        o_rw = _rwkv(u_rkv.reshape(bsz, seq, rkv_cols), u_lora.reshape(bsz, seq, lora_cols),
                     rw_params, rw_width, RW_SEQS if bsz % RW_SEQS == 0 else 1)

        post_weights = (bf(w_up_sb[i]), bf(w_up_rw[i]), bf(w_out[i]), vec(mlp_norm_g[i]),
                        bf(w_ff1[i]), bf(w_ff2[i]), vec(ple_norm_g[i]), bf(w_ple_gate[i]),
                        bf(w_ple_proj[i]), vec(final_norm_g))
        x2 = _post(x2, o_sb.reshape(tokens, sb_width), o_rw.reshape(tokens, rw_width), gates,
                   p[i].reshape(tokens, -1), post_weights, tm_post, final=(i == depth - 1))
    return x2.reshape(bsz, seq, d)
```

```python
import functools
import math

import jax
import jax.numpy as jnp
from jax import lax
from jax.experimental import pallas as pl
from jax.experimental.pallas import tpu as pltpu

NORM_EPS = 1e-6
GN_EPS = 64e-5
HEAD_DIM = 64
LANES = 128
SB_TILE = 256
SB_UNROLL = 12
SB_STREAMS = 1
RW_SEQS = 4
RW_CHUNK = 64
DECAY_LORA = 64
ICLR_LORA = 64
GATE_LORA = 128
VMEM_LIMIT = 56 * 1024 * 1024

F32 = jnp.float32
BF16 = jnp.bfloat16

_NT = (((1,), (1,)), ((), ()))


def _dot(a, b):
    return jnp.dot(a, b, preferred_element_type=F32)


def _dot_nt(a, b):
    return lax.dot_general(a, b, _NT, preferred_element_type=F32)


def _bdot(a, b):
    return lax.dot_general(a, b, (((2,), (1,)), ((0,), (0,))), preferred_element_type=F32)


def _bdot_nt(a, b):
    return lax.dot_general(a, b, (((2,), (2,)), ((0,), (0,))), preferred_element_type=F32)


def _split2(x):
    hi = x.astype(BF16)
    lo = (x - hi.astype(F32)).astype(BF16)
    return hi, lo


def _rms(x, g):
    return x * lax.rsqrt(jnp.mean(x * x, axis=-1, keepdims=True) + NORM_EPS) * g


def _resident(shape):
    nd = len(shape)
    return pl.BlockSpec(shape, lambda *_: (0,) * nd, pipeline_mode=pl.Buffered(1))


def _inproj_kernel(x_ref, g_ref, wsb_ref, wrkv_ref, wlora_ref, wgate_ref,
                   sb_ref, rkv_ref, lora_ref, gate_ref):
    hb = _rms(x_ref[...], g_ref[...]).astype(BF16)
    sb_ref[...] = _dot(hb, wsb_ref[...]).astype(sb_ref.dtype)
    rkv_ref[...] = _dot(hb, wrkv_ref[...])
    lora_ref[...] = _dot(hb, wlora_ref[...])
    gate_ref[...] = jax.nn.sigmoid(_dot(hb, wgate_ref[...])).astype(gate_ref.dtype)


def _inproj(x2, g, wsb, wrkv, wlora, wgate, tm):
    t, d = x2.shape
    row = lambda n: pl.BlockSpec((tm, n), lambda i: (i, 0))
    ws = (wsb, wrkv, wlora, wgate)
    return pl.pallas_call(
        _inproj_kernel,
        grid=(t // tm,),
        in_specs=[row(d), _resident(g.shape)] + [_resident(w.shape) for w in ws],
        out_specs=[row(w.shape[1]) for w in ws],
        out_shape=[jax.ShapeDtypeStruct((t, wsb.shape[1]), BF16),
                   jax.ShapeDtypeStruct((t, wrkv.shape[1]), F32),
                   jax.ShapeDtypeStruct((t, wlora.shape[1]), F32),
                   jax.ShapeDtypeStruct((t, wgate.shape[1]), BF16)],
        compiler_params=pltpu.CompilerParams(dimension_semantics=("parallel",),
                                             vmem_limit_bytes=VMEM_LIMIT),
        name="inproj",
    )(x2, g, *ws)


LOG2_E = math.log2(math.e)
SOFTPLUS_LINEAR = 30.0
MASKED = -1e30


def _sb_attn_kernel(q_ref, k_ref, v_ref, o_ref, hl_ref, z_ref, bias_ref, *, scale, tile):
    heads = LANES // HEAD_DIM
    rows = heads * tile
    n_tiles = q_ref.shape[1] // tile
    streams = q_ref.shape[2] // LANES
    lane = lax.broadcasted_iota(jnp.int32, (tile, LANES), 1)
    row = lax.broadcasted_iota(jnp.int32, (rows, tile), 0) % tile
    col = lax.broadcasted_iota(jnp.int32, (rows, tile), 1)
    bias_ref[0] = jnp.zeros((rows, tile), F32)
    bias_ref[1] = jnp.where(col < row, 0.0, MASKED)
    key_j = lax.broadcasted_iota(jnp.int32, (2 * tile, tile), 0) % tile
    key_s = lax.broadcasted_iota(jnp.int32, (2 * tile, tile), 1)
    from_key = jnp.where(key_j >= key_s, -1.0, 0.0).astype(BF16)

    def tile_rows(t):
        return pl.ds(pl.multiple_of(t * tile, tile), tile)

    def score_matmul(g, qt, kt):
        lanes = slice(g * LANES, (g + 1) * LANES)
        q = q_ref[0, tile_rows(qt), lanes] * scale
        zero = jnp.zeros_like(q)
        qh = jnp.concatenate([jnp.where(lane // HEAD_DIM == h, q, zero) for h in range(heads)], axis=0)
        return _dot_nt(qh, k_ref[0, tile_rows(kt), lanes])

    def store_scores(g, qt, kt, slot):
        z_ref[g, slot] = score_matmul(g, qt, kt) + bias_ref[(qt == kt).astype(jnp.int32)]

    def split_scores(g, slot):
        z = z_ref[g, slot]
        hi, lo = _split2(jnp.where(z > SOFTPLUS_LINEAR, z, jnp.log(1.0 + jnp.exp2(z * LOG2_E))))
        hl_ref[g, slot, :, :tile] = hi
        hl_ref[g, slot, :, tile:] = lo

    def weights(g, after, qt, kt, slot, c, acc):
        lanes = slice(g * LANES, (g + 1) * LANES)
        first = qt == kt
        c = jnp.where(first, 0.0, c)
        acc = jnp.where(first, 0.0, acc)
        w = jnp.exp(after + z_ref[g, slot] + c)
        acc = acc + _dot(w.astype(BF16), v_ref[0, tile_rows(kt), lanes])
        c = c + after[:, 0:1]
        out = acc[:tile]
        for h in range(1, heads):
            out = jnp.where(lane // HEAD_DIM == h, acc[h * tile:(h + 1) * tile], out)
        o_ref[0, tile_rows(qt), lanes] = out.astype(o_ref.dtype)
        return c, acc

    def following(qt, kt):
        last_key = kt == 0
        qt_next = jnp.where(last_key, qt + 1, qt)
        kt_next = jnp.where(last_key, qt + 1, kt - 1)
        return jnp.minimum(qt_next, n_tiles - 1), jnp.minimum(kt_next, n_tiles - 1)

    def step(carry, slot):
        p0, p1, state = carry
        p2 = following(*p1)
        after = [_dot(hl_ref[g, slot], from_key) for g in range(streams)]
        for g in range(streams):
            split_scores(g, (slot + 1) % 3)
        for g in range(streams):
            store_scores(g, *p2, (slot + 2) % 3)
        state = tuple(weights(g, after[g], *p0, slot, *state[g]) for g in range(streams))
        return p1, p2, state

    n_pairs = n_tiles * (n_tiles + 1) // 2
    zero_i = jnp.zeros((), jnp.int32)
    p0 = (zero_i, zero_i)
    p1 = following(*p0)
    for g in range(streams):
        store_scores(g, *p0, 0)
        split_scores(g, 0)
        store_scores(g, *p1, 1)
    state = tuple((jnp.zeros((rows, 1), F32), jnp.zeros((rows, LANES), F32)) for _ in range(streams))
    def trip(_, carry):
        for i in range(SB_UNROLL):
            carry = step(carry, i % 3)
        return carry

    carry = lax.fori_loop(0, n_pairs // SB_UNROLL, trip, (p0, p1, state))
    for i in range(n_pairs % SB_UNROLL):
        carry = step(carry, i % 3)


def _sb_attention(u_sb, width, tile, streams):
    b, s, _ = u_sb.shape
    blocks = width // (streams * LANES)
    rows = (LANES // HEAD_DIM) * tile
    kernel = functools.partial(_sb_attn_kernel, scale=1.0 / math.sqrt(HEAD_DIM), tile=tile)
    seq_block = lambda col0: pl.BlockSpec((1, s, streams * LANES), lambda bi, hp: (bi, 0, col0 + hp))
    return pl.pallas_call(
        kernel,
        grid=(b, blocks),
        in_specs=[seq_block(0), seq_block(blocks), seq_block(2 * blocks)],
        out_specs=seq_block(0),
        out_shape=jax.ShapeDtypeStruct((b, s, width), BF16),
        scratch_shapes=[pltpu.VMEM((streams, 3, rows, 2 * tile), BF16),
                        pltpu.VMEM((streams, 3, rows, tile), F32),
                        pltpu.VMEM((2, rows, tile), F32)],
        compiler_params=pltpu.CompilerParams(dimension_semantics=("parallel", "parallel"),
                                             vmem_limit_bytes=VMEM_LIMIT),
        name="sb_attention",
    )(u_sb, u_sb, u_sb)


def _rwkv_kernel(rkv_ref, lora_ref, mu_rkv_ref, mu_lora_ref, w0_ref, w2_ref, a0_ref, a2_ref,
                 g2_ref, kk_ref, ka_ref, rk_ref, lnw_ref, lnb_ref, o_ref,
                 state_ref, prev_rkv_ref, prev_lora_ref):
    nseq, c_len = rkv_ref.shape[0], rkv_ref.shape[1]
    width = o_ref.shape[2]
    pairs = width // LANES
    rows = nseq * c_len

    @pl.when(pl.program_id(1) == 0)
    def _():
        state_ref[...] = jnp.zeros_like(state_ref)
        prev_rkv_ref[...] = jnp.zeros_like(prev_rkv_ref)
        prev_lora_ref[...] = jnp.zeros_like(prev_lora_ref)

    def token_shift(u_ref, prev_ref, mu_ref):
        u = u_ref[...].reshape(rows, u_ref.shape[2])
        t = lax.broadcasted_iota(jnp.int32, u.shape, 0)
        prev = pltpu.roll(u, 1, 0)
        for s in range(nseq):
            prev = jnp.where(t == s * c_len, prev_ref[s], prev)
            prev_ref[s] = u[(s + 1) * c_len - 1:(s + 1) * c_len, :]
        return u + (prev - u) * mu_ref[...]

    rkv = token_shift(rkv_ref, prev_rkv_ref, mu_rkv_ref)
    lora = token_shift(lora_ref, prev_lora_ref, mu_lora_ref)
    r = rkv[:, :width]
    k = rkv[:, width:2 * width]
    v = rkv[:, 2 * width:]
    x_wa = lora[:, :DECAY_LORA + ICLR_LORA]
    x_g = lora[:, DECAY_LORA + ICLR_LORA:]

    d_w = w0_ref[...] + _dot(jnp.tanh(x_wa).astype(BF16), w2_ref[...])
    log_decay = jax.nn.sigmoid(d_w) * -math.exp(-0.5)
    iclr = jax.nn.sigmoid(a0_ref[...] + _dot(x_wa.astype(BF16), a2_ref[...]))
    gate = _dot(jax.nn.sigmoid(x_g).astype(BF16), g2_ref[...])

    lane_i = lax.broadcasted_iota(jnp.int32, (LANES, LANES), 0)
    lane_j = lax.broadcasted_iota(jnp.int32, (LANES, LANES), 1)
    same_head = lane_i // HEAD_DIM == lane_j // HEAD_DIM
    head_ones = jnp.concatenate([same_head, same_head], axis=0).astype(BF16)

    def head_sum(x):
        return _dot(jnp.concatenate(_split2(x.reshape(-1, LANES)), axis=1), head_ones).reshape(x.shape)

    def group(x):
        return jnp.stack([x[s * c_len:(s + 1) * c_len, p * LANES:(p + 1) * LANES]
                          for s in range(nseq) for p in range(pairs)])

    def group_param(ref):
        return jnp.stack([ref[:, p * LANES:(p + 1) * LANES] for _ in range(nseq) for p in range(pairs)])

    k_eff = group(k * (1.0 + (iclr - 1.0) * ka_ref[...]))
    kk = group(k * kk_ref[...])
    kk = kk * lax.rsqrt(jnp.maximum(head_sum(kk * kk), 1e-24))

    t_i = lax.broadcasted_iota(jnp.int32, (rows, rows), 0)
    t_j = lax.broadcasted_iota(jnp.int32, (rows, rows), 1)
    tri = ((t_j <= t_i) & (t_i // c_len == t_j // c_len)).astype(BF16)
    cum_flat = sum(_dot(tri, part) for part in _split2(log_decay))
    cum = group(cum_flat)
    w_prev = group(jnp.exp(cum_flat - log_decay))
    w_t = jnp.exp(cum)
    inv_w = jnp.exp(-cum)
    w_last = w_t[:, c_len - 1:c_len, :]
    r = group(r)
    v = group(v)
    iclr = group(iclr)
    r_t = r * w_t
    a_t = -kk * w_prev
    b_t = kk * iclr * inv_w
    k_t = k_eff * inv_w

    row = lax.broadcasted_iota(jnp.int32, (c_len, LANES), 0)
    col = lax.broadcasted_iota(jnp.int32, (c_len, LANES), 1) % HEAD_DIM
    strict = col < row
    incl = col <= row
    eye = (col == row).astype(F32)

    def bd(x):
        m = lax.broadcasted_iota(jnp.int32, x.shape, 2) < HEAD_DIM
        zero = jnp.zeros_like(x)
        return jnp.concatenate([jnp.where(m, x, zero), jnp.where(m, zero, x)], axis=1)

    lhs = jnp.concatenate([a_t, r_t], axis=1).astype(BF16)
    rhs = jnp.concatenate([bd(b_t.astype(BF16)), bd(k_t.astype(BF16))], axis=1)
    a_all = _bdot_nt(lhs, rhs)
    a_ab = jnp.where(strict, a_all[:, :c_len, :LANES], 0.0)
    a_ak = jnp.where(strict, a_all[:, :c_len, LANES:], 0.0)
    a_rb = jnp.where(incl, a_all[:, c_len:, :LANES], 0.0)
    a_rk = jnp.where(incl, a_all[:, c_len:, LANES:], 0.0)

    t_inv = eye + a_ab
    power = a_ab.astype(BF16)
    power = _bdot(power, bd(power))
    n = 4
    while n < c_len:
        both = _bdot(jnp.concatenate([t_inv, power], axis=1).astype(BF16), bd(power.astype(BF16)))
        t_inv = t_inv + both[:, :c_len]
        power = both[:, c_len:]
        n *= 2
    t_inv = t_inv + _bdot(t_inv.astype(BF16), bd(power.astype(BF16)))

    state = state_ref[...]
    bd_v = bd(v.astype(BF16))
    from_state = _bdot_nt(lhs, state.astype(BF16))
    u = _bdot(t_inv.astype(BF16),
              bd((from_state[:, :c_len] + _bdot(a_ak.astype(BF16), bd_v)).astype(BF16)))
    y = from_state[:, c_len:] + _bdot(
        jnp.concatenate([a_rb, a_rk], axis=2).astype(BF16),
        jnp.concatenate([bd(u.astype(BF16)), bd_v], axis=1))
    uv = jnp.concatenate([u, v], axis=1)
    bk = jnp.concatenate([b_t * w_last, k_t * w_last], axis=1)
    grown = _bdot(jnp.swapaxes(uv, 1, 2).astype(BF16), bk.astype(BF16))
    state_ref[...] = state * w_last + jnp.where(same_head, grown, 0.0)

    mean = head_sum(y) * (1.0 / HEAD_DIM)
    d = y - mean
    var = head_sum(d * d) * (1.0 / HEAD_DIM)
    y = d * lax.rsqrt(var + GN_EPS) * group_param(lnw_ref) + group_param(lnb_ref)
    y = y + head_sum(r * k_eff * group_param(rk_ref)) * v
    out = (y * group(gate)).astype(o_ref.dtype)
    for s in range(nseq):
        for p in range(pairs):
            o_ref[s, :, p * LANES:(p + 1) * LANES] = out[s * pairs + p]


def _rwkv(u_rkv, u_lora, params, width, nseq):
    b, s, _ = u_rkv.shape
    c_len = RW_CHUNK
    chunk = lambda n: pl.BlockSpec((nseq, c_len, n), lambda bi, ci: (bi, ci, 0))
    return pl.pallas_call(
        _rwkv_kernel,
        grid=(b // nseq, s // c_len),
        in_specs=[chunk(u_rkv.shape[2]), chunk(u_lora.shape[2])] + [_resident(a.shape) for a in params],
        out_specs=chunk(width),
        out_shape=jax.ShapeDtypeStruct((b, s, width), BF16),
        scratch_shapes=[pltpu.VMEM((nseq * (width // LANES), LANES, LANES), F32),
                        pltpu.VMEM((nseq, 1, u_rkv.shape[2]), F32),
                        pltpu.VMEM((nseq, 1, u_lora.shape[2]), F32)],
        compiler_params=pltpu.CompilerParams(dimension_semantics=("parallel", "arbitrary"),
                                             vmem_limit_bytes=VMEM_LIMIT),
        name="rwkv7",
    )(u_rkv, u_lora, *params)


def _post_kernel(x_ref, osb_ref, orw_ref, gate_ref, p_ref, wupsb_ref, wuprw_ref, wout_ref,
                 gmlp_ref, wff1_ref, wff2_ref, gple_ref, wpg_ref, wpp_ref, gfin_ref, o_ref,
                 *, ff_chunk, final):
    x = x_ref[...]
    d = x.shape[1]
    merged = (gate_ref[:, :d].astype(F32) * _dot(osb_ref[...], wupsb_ref[...])
              + gate_ref[:, d:].astype(F32) * _dot(orw_ref[...], wuprw_ref[...]))
    x = x + _dot(merged.astype(BF16), wout_ref[...])

    h = _rms(x, gmlp_ref[...]).astype(BF16)
    mlp = jnp.zeros_like(x)
    for f0 in range(0, wff1_ref.shape[1], ff_chunk):
        hid = jnp.maximum(_dot(h, wff1_ref[:, f0:f0 + ff_chunk]), 0.0)
        mlp = mlp + _dot((hid * hid).astype(BF16), wff2_ref[f0:f0 + ff_chunk, :])
    x = x + mlp

    g = jax.nn.sigmoid(_dot(_rms(x, gple_ref[...]).astype(BF16), wpg_ref[...]))
    x = x + g * _dot(p_ref[...].astype(BF16), wpp_ref[...])
    o_ref[...] = _rms(x, gfin_ref[...]) if final else x


def _post(x2, o_sb, o_rw, gates, p2, weights, tm, final):
    t, d = x2.shape
    row = lambda n: pl.BlockSpec((tm, n), lambda i: (i, 0))
    kernel = functools.partial(_post_kernel, ff_chunk=1024, final=final)
    return pl.pallas_call(
        kernel,
        grid=(t // tm,),
        in_specs=[row(d), row(o_sb.shape[1]), row(o_rw.shape[1]), row(gates.shape[1]), row(p2.shape[1])]
                 + [_resident(w.shape) for w in weights],
        out_specs=row(d),
        out_shape=jax.ShapeDtypeStruct((t, d), F32),
        compiler_params=pltpu.CompilerParams(dimension_semantics=("parallel",),
                                             vmem_limit_bytes=VMEM_LIMIT),
        name="post",
    )(x2, o_sb, o_rw, gates, p2, *weights)


def kernel(x, p, attn_norm_g, w_in, shift_mu, decay_w0, decay_w2, iclr_a0, iclr_a2, gate_g2, k_k, k_a, r_k, ln_x_w, ln_x_b, w_up_sb, w_up_rw, w_out, mlp_norm_g, w_ff1, w_ff2, ple_norm_g, w_ple_gate, w_ple_proj, final_norm_g):
    bsz, seq, d = x.shape
    depth = w_in.shape[0]
    sb_width = w_up_sb.shape[1]
    rw_width = w_up_rw.shape[1]
    sb_cols = 3 * sb_width
    rkv_cols = 3 * rw_width
    lora_cols = DECAY_LORA + ICLR_LORA + GATE_LORA
    rw_end = sb_cols + rkv_cols + lora_cols
    tokens = bsz * seq
    tm_in = min(512, tokens)
    tm_post = min(512, tokens)
    vec = lambda a: a.reshape(1, -1).astype(F32)
    bf = lambda a: a.astype(BF16)

    x2 = x.reshape(tokens, d)
    for i in range(depth):
        wi = w_in[i]
        u_sb, u_rkv, u_lora, gates = _inproj(
            x2, vec(attn_norm_g[i]), bf(wi[:, :sb_cols]), bf(wi[:, sb_cols:sb_cols + rkv_cols]),
            bf(wi[:, sb_cols + rkv_cols:rw_end]), bf(wi[:, rw_end:]), tm_in)

        o_sb = _sb_attention(u_sb.reshape(bsz, seq, sb_cols), sb_width, min(SB_TILE, seq), SB_STREAMS)

        mu = shift_mu[i]
        w2_pad = jnp.concatenate([decay_w2[i], jnp.zeros((ICLR_LORA, rw_width), F32)], axis=0)
        a2_pad = jnp.concatenate([jnp.zeros((DECAY_LORA, rw_width), F32), iclr_a2[i]], axis=0)
        rw_params = (
            vec(mu[:rkv_cols]), vec(mu[rkv_cols:]), vec(decay_w0[i]), bf(w2_pad), vec(iclr_a0[i]),
            bf(a2_pad), bf(gate_g2[i]), vec(k_k[i]), vec(k_a[i]), vec(r_k[i]), vec(ln_x_w[i]),
            vec(ln_x_b[i]))
        o_rw = _rwkv(u_rkv.reshape(bsz, seq, rkv_cols), u_lora.reshape(bsz, seq, lora_cols),
                     rw_params, rw_width, RW_SEQS if bsz % RW_SEQS == 0 else 1)

        post_weights = (bf(w_up_sb[i]), bf(w_up_rw[i]), bf(w_out[i]), vec(mlp_norm_g[i]),
                        bf(w_ff1[i]), bf(w_ff2[i]), vec(ple_norm_g[i]), bf(w_ple_gate[i]),
                        bf(w_ple_proj[i]), vec(final_norm_g))
        x2 = _post(x2, o_sb.reshape(tokens, sb_width), o_rw.reshape(tokens, rw_width), gates,
                   p[i].reshape(tokens, -1), post_weights, tm_post, final=(i == depth - 1))
    return x2.reshape(bsz, seq, d)
```

```python
import functools
import math

import jax
import jax.numpy as jnp
from jax import lax
from jax.experimental import pallas as pl
from jax.experimental.pallas import tpu as pltpu

NORM_EPS = 1e-6
GN_EPS = 64e-5
HEAD_DIM = 64
LANES = 128
SB_TILE = 256
RW_SEQS = 4
RW_CHUNK = 64
DECAY_LORA = 64
ICLR_LORA = 64
GATE_LORA = 128
VMEM_LIMIT = 56 * 1024 * 1024

F32 = jnp.float32
BF16 = jnp.bfloat16

_NT = (((1,), (1,)), ((), ()))


def _dot(a, b):
    return jnp.dot(a, b, preferred_element_type=F32)


def _dot_nt(a, b):
    return lax.dot_general(a, b, _NT, preferred_element_type=F32)


def _bdot(a, b):
    return lax.dot_general(a, b, (((2,), (1,)), ((0,), (0,))), preferred_element_type=F32)


def _bdot_nt(a, b):
    return lax.dot_general(a, b, (((2,), (2,)), ((0,), (0,))), preferred_element_type=F32)


def _split2(x):
    hi = x.astype(BF16)
    lo = (x - hi.astype(F32)).astype(BF16)
    return hi, lo


def _rms(x, g):
    return x * lax.rsqrt(jnp.mean(x * x, axis=-1, keepdims=True) + NORM_EPS) * g


def _resident(shape):
    nd = len(shape)
    return pl.BlockSpec(shape, lambda *_: (0,) * nd, pipeline_mode=pl.Buffered(1))


def _inproj_kernel(x_ref, g_ref, wsb_ref, wrkv_ref, wlora_ref, wgate_ref,
                   sb_ref, rkv_ref, lora_ref, gate_ref):
    hb = _rms(x_ref[...], g_ref[...]).astype(BF16)
    sb_ref[...] = _dot(hb, wsb_ref[...]).astype(sb_ref.dtype)
    rkv_ref[...] = _dot(hb, wrkv_ref[...])
    lora_ref[...] = _dot(hb, wlora_ref[...])
    gate_ref[...] = jax.nn.sigmoid(_dot(hb, wgate_ref[...])).astype(gate_ref.dtype)


def _inproj(x2, g, wsb, wrkv, wlora, wgate, tm):
    t, d = x2.shape
    row = lambda n: pl.BlockSpec((tm, n), lambda i: (i, 0))
    ws = (wsb, wrkv, wlora, wgate)
    return pl.pallas_call(
        _inproj_kernel,
        grid=(t // tm,),
        in_specs=[row(d), _resident(g.shape)] + [_resident(w.shape) for w in ws],
        out_specs=[row(w.shape[1]) for w in ws],
        out_shape=[jax.ShapeDtypeStruct((t, wsb.shape[1]), BF16),
                   jax.ShapeDtypeStruct((t, wrkv.shape[1]), F32),
                   jax.ShapeDtypeStruct((t, wlora.shape[1]), F32),
                   jax.ShapeDtypeStruct((t, wgate.shape[1]), BF16)],
        compiler_params=pltpu.CompilerParams(dimension_semantics=("parallel",),
                                             vmem_limit_bytes=VMEM_LIMIT),
        name="inproj",
    )(x2, g, *ws)


LOG2_E = math.log2(math.e)
SOFTPLUS_LINEAR = 30.0
MASKED = -1e30


def _sb_attn_kernel(q_ref, k_ref, v_ref, o_ref, hl_ref, z_ref, bias_ref, *, scale, tile):
    heads = LANES // HEAD_DIM
    rows = heads * tile
    n_tiles = q_ref.shape[1] // tile
    lane = lax.broadcasted_iota(jnp.int32, (tile, LANES), 1)
    row = lax.broadcasted_iota(jnp.int32, (rows, tile), 0) % tile
    col = lax.broadcasted_iota(jnp.int32, (rows, tile), 1)
    bias_ref[...] = jnp.where(col < row, 0.0, MASKED)
    key_j = lax.broadcasted_iota(jnp.int32, (2 * tile, tile), 0) % tile
    key_s = lax.broadcasted_iota(jnp.int32, (2 * tile, tile), 1)
    from_key = jnp.where(key_j >= key_s, -1.0, 0.0).astype(BF16)

    def tile_rows(t):
        return pl.ds(t * tile, tile)

    def store_scores(qt, kt, slot):
        q = q_ref[0, tile_rows(qt), :] * scale
        zero = jnp.zeros_like(q)
        qh = jnp.concatenate([jnp.where(lane // HEAD_DIM == h, q, zero) for h in range(heads)], axis=0)
        z = _dot_nt(qh, k_ref[0, tile_rows(kt), :])
        z_ref[slot] = z + bias_ref[...] if qt == kt else z

    def split_scores(slot):
        z = z_ref[slot]
        hi, lo = _split2(jnp.where(z > SOFTPLUS_LINEAR, z, jnp.log(1.0 + jnp.exp2(z * LOG2_E))))
        hl_ref[slot, :, :tile] = hi
        hl_ref[slot, :, tile:] = lo

    def weights(qt, kt, slot, c, acc):
        after = _dot(hl_ref[slot], from_key)
        if qt == kt:
            w = jnp.exp(after + z_ref[slot])
            acc = _dot(w.astype(BF16), v_ref[0, tile_rows(kt), :])
            c = after[:, 0:1]
        else:
            w = jnp.exp(after + z_ref[slot] + c)
            acc = acc + _dot(w.astype(BF16), v_ref[0, tile_rows(kt), :])
            c = c + after[:, 0:1]
        if kt == 0:
            out = acc[:tile]
            for h in range(1, heads):
                out = jnp.where(lane // HEAD_DIM == h, acc[h * tile:(h + 1) * tile], out)
            o_ref[0, tile_rows(qt), :] = out.astype(o_ref.dtype)
        return c, acc

    pairs = [(qt, kt) for qt in range(n_tiles) for kt in range(qt, -1, -1)]
    store_scores(*pairs[0], 0)
    split_scores(0)
    if len(pairs) > 1:
        store_scores(*pairs[1], 1)
    c = acc = None
    for n, pair in enumerate(pairs):
        if n + 1 < len(pairs):
            split_scores((n + 1) % 3)
        if n + 2 < len(pairs):
            store_scores(*pairs[n + 2], (n + 2) % 3)
        c, acc = weights(*pair, n % 3, c, acc)


def _sb_attention(u_sb, width, tile):
    b, s, _ = u_sb.shape
    blocks = width // LANES
    rows = (LANES // HEAD_DIM) * tile
    kernel = functools.partial(_sb_attn_kernel, scale=1.0 / math.sqrt(HEAD_DIM), tile=tile)
    seq_block = lambda col0: pl.BlockSpec((1, s, LANES), lambda bi, hp: (bi, 0, col0 + hp))
    return pl.pallas_call(
        kernel,
        grid=(b, blocks),
        in_specs=[seq_block(0), seq_block(blocks), seq_block(2 * blocks)],
        out_specs=seq_block(0),
        out_shape=jax.ShapeDtypeStruct((b, s, width), BF16),
        scratch_shapes=[pltpu.VMEM((3, rows, 2 * tile), BF16),
                        pltpu.VMEM((3, rows, tile), F32),
                        pltpu.VMEM((rows, tile), F32)],
        compiler_params=pltpu.CompilerParams(dimension_semantics=("parallel", "parallel"),
                                             vmem_limit_bytes=VMEM_LIMIT),
        name="sb_attention",
    )(u_sb, u_sb, u_sb)


def _rwkv_kernel(rkv_ref, lora_ref, mu_rkv_ref, mu_lora_ref, w0_ref, w2_ref, a0_ref, a2_ref,
                 g2_ref, kk_ref, ka_ref, rk_ref, lnw_ref, lnb_ref, o_ref,
                 state_ref, prev_rkv_ref, prev_lora_ref):
    nseq, c_len = rkv_ref.shape[0], rkv_ref.shape[1]
    width = o_ref.shape[2]
    pairs = width // LANES
    rows = nseq * c_len

    @pl.when(pl.program_id(1) == 0)
    def _():
        state_ref[...] = jnp.zeros_like(state_ref)
        prev_rkv_ref[...] = jnp.zeros_like(prev_rkv_ref)
        prev_lora_ref[...] = jnp.zeros_like(prev_lora_ref)

    def token_shift(u_ref, prev_ref, mu_ref):
        u = u_ref[...].reshape(rows, u_ref.shape[2])
        t = lax.broadcasted_iota(jnp.int32, u.shape, 0)
        prev = pltpu.roll(u, 1, 0)
        for s in range(nseq):
            prev = jnp.where(t == s * c_len, prev_ref[s], prev)
            prev_ref[s] = u[(s + 1) * c_len - 1:(s + 1) * c_len, :]
        return u + (prev - u) * mu_ref[...]

    rkv = token_shift(rkv_ref, prev_rkv_ref, mu_rkv_ref)
    lora = token_shift(lora_ref, prev_lora_ref, mu_lora_ref)
    r = rkv[:, :width]
    k = rkv[:, width:2 * width]
    v = rkv[:, 2 * width:]
    x_wa = lora[:, :DECAY_LORA + ICLR_LORA]
    x_g = lora[:, DECAY_LORA + ICLR_LORA:]

    d_w = w0_ref[...] + _dot(jnp.tanh(x_wa).astype(BF16), w2_ref[...])
    log_decay = jax.nn.sigmoid(d_w) * -math.exp(-0.5)
    iclr = jax.nn.sigmoid(a0_ref[...] + _dot(x_wa.astype(BF16), a2_ref[...]))
    gate = _dot(jax.nn.sigmoid(x_g).astype(BF16), g2_ref[...])

    lane_i = lax.broadcasted_iota(jnp.int32, (LANES, LANES), 0)
    lane_j = lax.broadcasted_iota(jnp.int32, (LANES, LANES), 1)
    same_head = lane_i // HEAD_DIM == lane_j // HEAD_DIM
    head_ones = jnp.concatenate([same_head, same_head], axis=0).astype(BF16)

    def head_sum(x):
        return _dot(jnp.concatenate(_split2(x.reshape(-1, LANES)), axis=1), head_ones).reshape(x.shape)

    def group(x):
        return jnp.stack([x[s * c_len:(s + 1) * c_len, p * LANES:(p + 1) * LANES]
                          for s in range(nseq) for p in range(pairs)])

    def group_param(ref):
        return jnp.stack([ref[:, p * LANES:(p + 1) * LANES] for _ in range(nseq) for p in range(pairs)])

    k_eff = group(k * (1.0 + (iclr - 1.0) * ka_ref[...]))
    kk = group(k * kk_ref[...])
    kk = kk * lax.rsqrt(jnp.maximum(head_sum(kk * kk), 1e-24))

    t_i = lax.broadcasted_iota(jnp.int32, (rows, rows), 0)
    t_j = lax.broadcasted_iota(jnp.int32, (rows, rows), 1)
    tri = ((t_j <= t_i) & (t_i // c_len == t_j // c_len)).astype(BF16)
    cum_flat = sum(_dot(tri, part) for part in _split2(log_decay))
    cum = group(cum_flat)
    w_prev = group(jnp.exp(cum_flat - log_decay))
    w_t = jnp.exp(cum)
    inv_w = jnp.exp(-cum)
    w_last = w_t[:, c_len - 1:c_len, :]
    r = group(r)
    v = group(v)
    iclr = group(iclr)
    r_t = r * w_t
    a_t = -kk * w_prev
    b_t = kk * iclr * inv_w
    k_t = k_eff * inv_w

    row = lax.broadcasted_iota(jnp.int32, (c_len, LANES), 0)
    col = lax.broadcasted_iota(jnp.int32, (c_len, LANES), 1) % HEAD_DIM
    strict = col < row
    incl = col <= row
    eye = (col == row).astype(F32)

    def bd(x):
        m = lax.broadcasted_iota(jnp.int32, x.shape, 2) < HEAD_DIM
        zero = jnp.zeros_like(x)
        return jnp.concatenate([jnp.where(m, x, zero), jnp.where(m, zero, x)], axis=1)

    lhs = jnp.concatenate([a_t, r_t], axis=1).astype(BF16)
    rhs = jnp.concatenate([bd(b_t.astype(BF16)), bd(k_t.astype(BF16))], axis=1)
    a_all = _bdot_nt(lhs, rhs)
    a_ab = jnp.where(strict, a_all[:, :c_len, :LANES], 0.0)
    a_ak = jnp.where(strict, a_all[:, :c_len, LANES:], 0.0)
    a_rb = jnp.where(incl, a_all[:, c_len:, :LANES], 0.0)
    a_rk = jnp.where(incl, a_all[:, c_len:, LANES:], 0.0)

    t_inv = eye + a_ab
    power = a_ab.astype(BF16)
    power = _bdot(power, bd(power))
    n = 4
    while n < c_len:
        both = _bdot(jnp.concatenate([t_inv, power], axis=1).astype(BF16), bd(power.astype(BF16)))
        t_inv = t_inv + both[:, :c_len]
        power = both[:, c_len:]
        n *= 2
    t_inv = t_inv + _bdot(t_inv.astype(BF16), bd(power.astype(BF16)))

    state = state_ref[...]
    bd_v = bd(v.astype(BF16))
    from_state = _bdot_nt(lhs, state.astype(BF16))
    u = _bdot(t_inv.astype(BF16),
              bd((from_state[:, :c_len] + _bdot(a_ak.astype(BF16), bd_v)).astype(BF16)))
    y = from_state[:, c_len:] + _bdot(
        jnp.concatenate([a_rb, a_rk], axis=2).astype(BF16),
        jnp.concatenate([bd(u.astype(BF16)), bd_v], axis=1))
    uv = jnp.concatenate([u, v], axis=1)
    bk = jnp.concatenate([b_t * w_last, k_t * w_last], axis=1)
    grown = _bdot(jnp.swapaxes(uv, 1, 2).astype(BF16), bk.astype(BF16))
    state_ref[...] = state * w_last + jnp.where(same_head, grown, 0.0)

    mean = head_sum(y) * (1.0 / HEAD_DIM)
    d = y - mean
    var = head_sum(d * d) * (1.0 / HEAD_DIM)
    y = d * lax.rsqrt(var + GN_EPS) * group_param(lnw_ref) + group_param(lnb_ref)
    y = y + head_sum(r * k_eff * group_param(rk_ref)) * v
    out = (y * group(gate)).astype(o_ref.dtype)
    for s in range(nseq):
        for p in range(pairs):
            o_ref[s, :, p * LANES:(p + 1) * LANES] = out[s * pairs + p]


def _rwkv(u_rkv, u_lora, params, width, nseq):
    b, s, _ = u_rkv.shape
    c_len = RW_CHUNK
    chunk = lambda n: pl.BlockSpec((nseq, c_len, n), lambda bi, ci: (bi, ci, 0))
    return pl.pallas_call(
        _rwkv_kernel,
        grid=(b // nseq, s // c_len),
        in_specs=[chunk(u_rkv.shape[2]), chunk(u_lora.shape[2])] + [_resident(a.shape) for a in params],
        out_specs=chunk(width),
        out_shape=jax.ShapeDtypeStruct((b, s, width), BF16),
        scratch_shapes=[pltpu.VMEM((nseq * (width // LANES), LANES, LANES), F32),
                        pltpu.VMEM((nseq, 1, u_rkv.shape[2]), F32),
                        pltpu.VMEM((nseq, 1, u_lora.shape[2]), F32)],
        compiler_params=pltpu.CompilerParams(dimension_semantics=("parallel", "arbitrary"),
                                             vmem_limit_bytes=VMEM_LIMIT),
        name="rwkv7",
    )(u_rkv, u_lora, *params)


def _post_kernel(x_ref, osb_ref, orw_ref, gate_ref, p_ref, wupsb_ref, wuprw_ref, wout_ref,
                 gmlp_ref, wff1_ref, wff2_ref, gple_ref, wpg_ref, wpp_ref, gfin_ref, o_ref,
                 *, ff_chunk, final):
    x = x_ref[...]
    d = x.shape[1]
    merged = (gate_ref[:, :d].astype(F32) * _dot(osb_ref[...], wupsb_ref[...])
              + gate_ref[:, d:].astype(F32) * _dot(orw_ref[...], wuprw_ref[...]))
    x = x + _dot(merged.astype(BF16), wout_ref[...])

    h = _rms(x, gmlp_ref[...]).astype(BF16)
    mlp = jnp.zeros_like(x)
    for f0 in range(0, wff1_ref.shape[1], ff_chunk):
        hid = jnp.maximum(_dot(h, wff1_ref[:, f0:f0 + ff_chunk]), 0.0)
        mlp = mlp + _dot((hid * hid).astype(BF16), wff2_ref[f0:f0 + ff_chunk, :])
    x = x + mlp

    g = jax.nn.sigmoid(_dot(_rms(x, gple_ref[...]).astype(BF16), wpg_ref[...]))
    x = x + g * _dot(p_ref[...].astype(BF16), wpp_ref[...])
    o_ref[...] = _rms(x, gfin_ref[...]) if final else x


def _post(x2, o_sb, o_rw, gates, p2, weights, tm, final):
    t, d = x2.shape
    row = lambda n: pl.BlockSpec((tm, n), lambda i: (i, 0))
    kernel = functools.partial(_post_kernel, ff_chunk=1024, final=final)
    return pl.pallas_call(
        kernel,
        grid=(t // tm,),
        in_specs=[row(d), row(o_sb.shape[1]), row(o_rw.shape[1]), row(gates.shape[1]), row(p2.shape[1])]
                 + [_resident(w.shape) for w in weights],
        out_specs=row(d),
        out_shape=jax.ShapeDtypeStruct((t, d), F32),
        compiler_params=pltpu.CompilerParams(dimension_semantics=("parallel",),
                                             vmem_limit_bytes=VMEM_LIMIT),
        name="post",
    )(x2, o_sb, o_rw, gates, p2, *weights)


def kernel(x, p, attn_norm_g, w_in, shift_mu, decay_w0, decay_w2, iclr_a0, iclr_a2, gate_g2, k_k, k_a, r_k, ln_x_w, ln_x_b, w_up_sb, w_up_rw, w_out, mlp_norm_g, w_ff1, w_ff2, ple_norm_g, w_ple_gate, w_ple_proj, final_norm_g):
    bsz, seq, d = x.shape
    depth = w_in.shape[0]
    sb_width = w_up_sb.shape[1]
    rw_width = w_up_rw.shape[1]
    sb_cols = 3 * sb_width
    rkv_cols = 3 * rw_width
    lora_cols = DECAY_LORA + ICLR_LORA + GATE_LORA
    rw_end = sb_cols + rkv_cols + lora_cols
    tokens = bsz * seq
    tm_in = min(512, tokens)
    tm_post = min(512, tokens)
    vec = lambda a: a.reshape(1, -1).astype(F32)
    bf = lambda a: a.astype(BF16)

    x2 = x.reshape(tokens, d)
    for i in range(depth):
        wi = w_in[i]
        u_sb, u_rkv, u_lora, gates = _inproj(
            x2, vec(attn_norm_g[i]), bf(wi[:, :sb_cols]), bf(wi[:, sb_cols:sb_cols + rkv_cols]),
            bf(wi[:, sb_cols + rkv_cols:rw_end]), bf(wi[:, rw_end:]), tm_in)

        o_sb = _sb_attention(u_sb.reshape(bsz, seq, sb_cols), sb_width, min(SB_TILE, seq))

        mu = shift_mu[i]
        w2_pad = jnp.concatenate([decay_w2[i], jnp.zeros((ICLR_LORA, rw_width), F32)], axis=0)
        a2_pad = jnp.concatenate([jnp.zeros((DECAY_LORA, rw_width), F32), iclr_a2[i]], axis=0)
        rw_params = (
            vec(mu[:rkv_cols]), vec(mu[rkv_cols:]), vec(decay_w0[i]), bf(w2_pad), vec(iclr_a0[i]),
            bf(a2_pad), bf(gate_g2[i]), vec(k_k[i]), vec(k_a[i]), vec(r_k[i]), vec(ln_x_w[i]),
            vec(ln_x_b[i]))
        o_rw = _rwkv(u_rkv.reshape(bsz, seq, rkv_cols), u_lora.reshape(bsz, seq, lora_cols),
                     rw_params, rw_width, RW_SEQS if bsz % RW_SEQS == 0 else 1)

        post_weights = (bf(w_up_sb[i]), bf(w_up_rw[i]), bf(w_out[i]), vec(mlp_norm_g[i]),
                        bf(w_ff1[i]), bf(w_ff2[i]), vec(ple_norm_g[i]), bf(w_ple_gate[i]),
                        bf(w_ple_proj[i]), vec(final_norm_g))
        x2 = _post(x2, o_sb.reshape(tokens, sb_width), o_rw.reshape(tokens, rw_width), gates,
                   p[i].reshape(tokens, -1), post_weights, tm_post, final=(i == depth - 1))
    return x2.reshape(bsz, seq, d)
```

```python
import functools
import math

import jax
import jax.numpy as jnp
from jax import lax
from jax.experimental import pallas as pl
from jax.experimental.pallas import tpu as pltpu

NORM_EPS = 1e-6
GN_EPS = 64e-5
HEAD_DIM = 64
LANES = 128
SB_TILE = 256
RW_SEQS = 16
RW_STREAM_SEQS = 4
RW_CHUNK = 64
DECAY_LORA = 64
ICLR_LORA = 64
GATE_LORA = 128
VMEM_LIMIT = 56 * 1024 * 1024

F32 = jnp.float32
BF16 = jnp.bfloat16

_NT = (((1,), (1,)), ((), ()))


def _dot(a, b):
    return jnp.dot(a, b, preferred_element_type=F32)


def _dot_nt(a, b):
    return lax.dot_general(a, b, _NT, preferred_element_type=F32)


def _bdot(a, b):
    return lax.dot_general(a, b, (((2,), (1,)), ((0,), (0,))), preferred_element_type=F32)


def _bdot_nt(a, b):
    return lax.dot_general(a, b, (((2,), (2,)), ((0,), (0,))), preferred_element_type=F32)


def _split2(x):
    hi = x.astype(BF16)
    lo = (x - hi.astype(F32)).astype(BF16)
    return hi, lo


def _rms(x, g):
    return x * lax.rsqrt(jnp.mean(x * x, axis=-1, keepdims=True) + NORM_EPS) * g


def _resident(shape):
    nd = len(shape)
    return pl.BlockSpec(shape, lambda *_: (0,) * nd, pipeline_mode=pl.Buffered(1))


def _inproj_kernel(x_ref, g_ref, wsb_ref, wrkv_ref, wlora_ref, wgate_ref,
                   sb_ref, rkv_ref, lora_ref, gate_ref):
    hb = _rms(x_ref[...], g_ref[...]).astype(BF16)
    gate_ref[...] = jax.nn.sigmoid(_dot(hb, wgate_ref[...])).astype(gate_ref.dtype)
    sb_ref[...] = _dot(hb, wsb_ref[...]).astype(sb_ref.dtype)
    rkv_ref[...] = _dot(hb, wrkv_ref[...])
    lora_ref[...] = _dot(hb, wlora_ref[...])


def _inproj(x2, g, wsb, wrkv, wlora, wgate, tm):
    t, d = x2.shape
    row = lambda n: pl.BlockSpec((tm, n), lambda i: (i, 0))
    ws = (wsb, wrkv, wlora, wgate)
    return pl.pallas_call(
        _inproj_kernel,
        grid=(t // tm,),
        in_specs=[row(d), _resident(g.shape)] + [_resident(w.shape) for w in ws],
        out_specs=[row(w.shape[1]) for w in ws],
        out_shape=[jax.ShapeDtypeStruct((t, wsb.shape[1]), BF16),
                   jax.ShapeDtypeStruct((t, wrkv.shape[1]), F32),
                   jax.ShapeDtypeStruct((t, wlora.shape[1]), F32),
                   jax.ShapeDtypeStruct((t, wgate.shape[1]), BF16)],
        compiler_params=pltpu.CompilerParams(dimension_semantics=("parallel",),
                                             vmem_limit_bytes=VMEM_LIMIT),
        name="inproj",
    )(x2, g, *ws)


LOG2_E = math.log2(math.e)
SOFTPLUS_LINEAR = 30.0
MASKED = -1e30


def _sb_attn_kernel(q_ref, k_ref, v_ref, o_ref, hl_ref, z_ref, bias_ref, *, scale, tile):
    heads = LANES // HEAD_DIM
    rows = heads * tile
    n_tiles = q_ref.shape[1] // tile
    lane = lax.broadcasted_iota(jnp.int32, (tile, LANES), 1)
    row = lax.broadcasted_iota(jnp.int32, (rows, tile), 0) % tile
    col = lax.broadcasted_iota(jnp.int32, (rows, tile), 1)
    bias_ref[...] = jnp.where(col < row, 0.0, MASKED)
    key_j = lax.broadcasted_iota(jnp.int32, (2 * tile, tile), 0) % tile
    key_s = lax.broadcasted_iota(jnp.int32, (2 * tile, tile), 1)
    from_key = jnp.where(key_j >= key_s, -1.0, 0.0).astype(BF16)

    def tile_rows(t):
        return pl.ds(t * tile, tile)

    def store_scores(qt, kt, slot):
        q = q_ref[0, tile_rows(qt), :] * scale
        zero = jnp.zeros_like(q)
        qh = jnp.concatenate([jnp.where(lane // HEAD_DIM == h, q, zero) for h in range(heads)], axis=0)
        z = _dot_nt(qh, k_ref[0, tile_rows(kt), :])
        z_ref[slot] = z + bias_ref[...] if qt == kt else z

    def split_scores(slot):
        z = z_ref[slot]
        hi, lo = _split2(jnp.where(z > SOFTPLUS_LINEAR, z, jnp.log(1.0 + jnp.exp2(z * LOG2_E))))
        hl_ref[slot, :, :tile] = hi
        hl_ref[slot, :, tile:] = lo

    def weights(qt, kt, slot, c, acc):
        after = _dot(hl_ref[slot], from_key)
        if qt == kt:
            w = jnp.exp(after + z_ref[slot])
            acc = _dot(w.astype(BF16), v_ref[0, tile_rows(kt), :])
            c = after[:, 0:1]
        else:
            w = jnp.exp(after + z_ref[slot] + c)
            acc = acc + _dot(w.astype(BF16), v_ref[0, tile_rows(kt), :])
            c = c + after[:, 0:1]
        if kt == 0:
            out = acc[:tile]
            for h in range(1, heads):
                out = jnp.where(lane // HEAD_DIM == h, acc[h * tile:(h + 1) * tile], out)
            o_ref[0, tile_rows(qt), :] = out.astype(o_ref.dtype)
        return c, acc

    pairs = [(qt, kt) for qt in range(n_tiles) for kt in range(qt, -1, -1)]
    store_scores(*pairs[0], 0)
    split_scores(0)
    if len(pairs) > 1:
        store_scores(*pairs[1], 1)
    c = acc = None
    for n, pair in enumerate(pairs):
        if n + 1 < len(pairs):
            split_scores((n + 1) % 3)
        if n + 2 < len(pairs):
            store_scores(*pairs[n + 2], (n + 2) % 3)
        c, acc = weights(*pair, n % 3, c, acc)


def _sb_attention(u_sb, width, tile):
    b, s, _ = u_sb.shape
    blocks = width // LANES
    rows = (LANES // HEAD_DIM) * tile
    kernel = functools.partial(_sb_attn_kernel, scale=1.0 / math.sqrt(HEAD_DIM), tile=tile)
    seq_block = lambda col0: pl.BlockSpec((1, s, LANES), lambda bi, hp: (bi, 0, col0 + hp))
    return pl.pallas_call(
        kernel,
        grid=(b, blocks),
        in_specs=[seq_block(0), seq_block(blocks), seq_block(2 * blocks)],
        out_specs=seq_block(0),
        out_shape=jax.ShapeDtypeStruct((b, s, width), BF16),
        scratch_shapes=[pltpu.VMEM((3, rows, 2 * tile), BF16),
                        pltpu.VMEM((3, rows, tile), F32),
                        pltpu.VMEM((rows, tile), F32)],
        compiler_params=pltpu.CompilerParams(dimension_semantics=("parallel", "parallel"),
                                             vmem_limit_bytes=VMEM_LIMIT),
        name="sb_attention",
    )(u_sb, u_sb, u_sb)


def _advance(gen):
    return next(gen, None)


def _rwkv_kernel(rkv_ref, lora_ref, *refs):
    params, (o_ref, state_ref, prev_rkv_ref, prev_lora_ref) = refs[:-4], refs[-4:]
    pairs = o_ref.shape[2] // LANES

    @pl.when(pl.program_id(1) == 0)
    def _():
        state_ref[...] = jnp.zeros_like(state_ref)
        prev_rkv_ref[...] = jnp.zeros_like(prev_rkv_ref)
        prev_lora_ref[...] = jnp.zeros_like(prev_lora_ref)

    nseq = rkv_ref.shape[0]
    per = math.gcd(nseq, RW_STREAM_SEQS)
    streams = []
    for s0 in range(0, nseq, per):
        seqs = slice(s0, s0 + per)
        groups = slice(s0 * pairs, (s0 + per) * pairs)
        streams.append(_rwkv_stream(rkv_ref.at[seqs], lora_ref.at[seqs], *params, o_ref.at[seqs],
                                    state_ref.at[groups], prev_rkv_ref.at[seqs], prev_lora_ref.at[seqs]))
    while _advance(streams[0]) != "front end done":
        pass
    for i, current in enumerate(streams):
        following = streams[i + 1] if i + 1 < len(streams) else None
        running, front = True, following is not None
        while running or front:
            if running:
                running = _advance(current) is not None
            if front:
                front = _advance(following) != "front end done"


def _rwkv_stream(rkv_ref, lora_ref, mu_rkv_ref, mu_lora_ref, w0_ref, w2_ref, a0_ref, a2_ref,
                 g2_ref, kk_ref, ka_ref, rk_ref, lnw_ref, lnb_ref, o_ref,
                 state_ref, prev_rkv_ref, prev_lora_ref):
    nseq, c_len = rkv_ref.shape[0], rkv_ref.shape[1]
    width = o_ref.shape[2]
    pairs = width // LANES
    rows = nseq * c_len

    def token_shift(u_ref, prev_ref, mu_ref):
        u = u_ref[...].reshape(rows, u_ref.shape[2])
        t = lax.broadcasted_iota(jnp.int32, u.shape, 0)
        prev = pltpu.roll(u, 1, 0)
        for s in range(nseq):
            prev = jnp.where(t == s * c_len, prev_ref[s], prev)
            prev_ref[s] = u[(s + 1) * c_len - 1:(s + 1) * c_len, :]
        return u + (prev - u) * mu_ref[...]

    rkv = token_shift(rkv_ref, prev_rkv_ref, mu_rkv_ref)
    yield "front end"
    lora = token_shift(lora_ref, prev_lora_ref, mu_lora_ref)
    r = rkv[:, :width]
    k = rkv[:, width:2 * width]
    v = rkv[:, 2 * width:]
    x_wa = lora[:, :DECAY_LORA + ICLR_LORA]
    x_g = lora[:, DECAY_LORA + ICLR_LORA:]

    d_w = w0_ref[...] + _dot(jnp.tanh(x_wa).astype(BF16), w2_ref[...])
    log_decay = jax.nn.sigmoid(d_w) * -math.exp(-0.5)
    iclr = jax.nn.sigmoid(a0_ref[...] + _dot(x_wa.astype(BF16), a2_ref[...]))
    gate = _dot(jax.nn.sigmoid(x_g).astype(BF16), g2_ref[...])
    yield "front end"

    lane_i = lax.broadcasted_iota(jnp.int32, (LANES, LANES), 0)
    lane_j = lax.broadcasted_iota(jnp.int32, (LANES, LANES), 1)
    same_head = lane_i // HEAD_DIM == lane_j // HEAD_DIM
    head_ones = jnp.concatenate([same_head, same_head], axis=0).astype(BF16)

    def head_sum(x):
        return _dot(jnp.concatenate(_split2(x.reshape(-1, LANES)), axis=1), head_ones).reshape(x.shape)

    def group(x):
        return jnp.stack([x[s * c_len:(s + 1) * c_len, p * LANES:(p + 1) * LANES]
                          for s in range(nseq) for p in range(pairs)])

    def group_param(ref):
        return jnp.stack([ref[:, p * LANES:(p + 1) * LANES] for _ in range(nseq) for p in range(pairs)])

    k_eff = group(k * (1.0 + (iclr - 1.0) * ka_ref[...]))
    kk = group(k * kk_ref[...])
    kk = kk * lax.rsqrt(jnp.maximum(head_sum(kk * kk), 1e-24))
    yield "front end"

    t_i = lax.broadcasted_iota(jnp.int32, (rows, rows), 0)
    t_j = lax.broadcasted_iota(jnp.int32, (rows, rows), 1)
    tri = ((t_j <= t_i) & (t_i // c_len == t_j // c_len)).astype(BF16)
    cum_flat = sum(_dot(tri, part) for part in _split2(log_decay))
    cum = group(cum_flat)
    w_prev = group(jnp.exp(cum_flat - log_decay))
    yield "front end"
    w_t = jnp.exp(cum)
    inv_w = jnp.exp(-cum)
    w_last = w_t[:, c_len - 1:c_len, :]
    r = group(r)
    v = group(v)
    iclr = group(iclr)
    r_t = r * w_t
    a_t = -kk * w_prev
    b_t = kk * iclr * inv_w
    k_t = k_eff * inv_w
    yield "front end done"

    row = lax.broadcasted_iota(jnp.int32, (c_len, LANES), 0)
    col = lax.broadcasted_iota(jnp.int32, (c_len, LANES), 1) % HEAD_DIM
    strict = col < row
    incl = col <= row
    eye = (col == row).astype(F32)

    def bd(x):
        m = lax.broadcasted_iota(jnp.int32, x.shape, 2) < HEAD_DIM
        zero = jnp.zeros_like(x)
        return jnp.concatenate([jnp.where(m, x, zero), jnp.where(m, zero, x)], axis=1)

    lhs = jnp.concatenate([a_t, r_t], axis=1).astype(BF16)
    rhs = jnp.concatenate([bd(b_t.astype(BF16)), bd(k_t.astype(BF16))], axis=1)
    a_all = _bdot_nt(lhs, rhs)
    a_ab = jnp.where(strict, a_all[:, :c_len, :LANES], 0.0)
    a_ak = jnp.where(strict, a_all[:, :c_len, LANES:], 0.0)
    a_rb = jnp.where(incl, a_all[:, c_len:, :LANES], 0.0)
    a_rk = jnp.where(incl, a_all[:, c_len:, LANES:], 0.0)
    yield "chain"

    t_inv = eye + a_ab
    power = a_ab.astype(BF16)
    power = _bdot(power, bd(power))
    yield "chain"
    n = 4
    while n < c_len:
        both = _bdot(jnp.concatenate([t_inv, power], axis=1).astype(BF16), bd(power.astype(BF16)))
        t_inv = t_inv + both[:, :c_len]
        power = both[:, c_len:]
        yield "chain"
        n *= 2
    t_inv = t_inv + _bdot(t_inv.astype(BF16), bd(power.astype(BF16)))
    yield "chain"

    state = state_ref[...]
    bd_v = bd(v.astype(BF16))
    from_state = _bdot_nt(lhs, state.astype(BF16))
    yield "chain"
    u = _bdot(t_inv.astype(BF16),
              bd((from_state[:, :c_len] + _bdot(a_ak.astype(BF16), bd_v)).astype(BF16)))
    yield "chain"
    y = from_state[:, c_len:] + _bdot(
        jnp.concatenate([a_rb, a_rk], axis=2).astype(BF16),
        jnp.concatenate([bd(u.astype(BF16)), bd_v], axis=1))
    yield "chain"
    uv = jnp.concatenate([u, v], axis=1)
    bk = jnp.concatenate([b_t * w_last, k_t * w_last], axis=1)
    grown = _bdot(jnp.swapaxes(uv, 1, 2).astype(BF16), bk.astype(BF16))
    state_ref[...] = state * w_last + jnp.where(same_head, grown, 0.0)
    yield "chain"

    mean = head_sum(y) * (1.0 / HEAD_DIM)
    d = y - mean
    var = head_sum(d * d) * (1.0 / HEAD_DIM)
    y = d * lax.rsqrt(var + GN_EPS) * group_param(lnw_ref) + group_param(lnb_ref)
    y = y + head_sum(r * k_eff * group_param(rk_ref)) * v
    out = (y * group(gate)).astype(o_ref.dtype)
    for s in range(nseq):
        for p in range(pairs):
            o_ref[s, :, p * LANES:(p + 1) * LANES] = out[s * pairs + p]


def _rwkv(u_rkv, u_lora, params, width, nseq):
    b, s, _ = u_rkv.shape
    c_len = RW_CHUNK
    chunk = lambda n: pl.BlockSpec((nseq, c_len, n), lambda bi, ci: (bi, ci, 0))
    return pl.pallas_call(
        _rwkv_kernel,
        grid=(b // nseq, s // c_len),
        in_specs=[chunk(u_rkv.shape[2]), chunk(u_lora.shape[2])] + [_resident(a.shape) for a in params],
        out_specs=chunk(width),
        out_shape=jax.ShapeDtypeStruct((b, s, width), BF16),
        scratch_shapes=[pltpu.VMEM((nseq * (width // LANES), LANES, LANES), F32),
                        pltpu.VMEM((nseq, 1, u_rkv.shape[2]), F32),
                        pltpu.VMEM((nseq, 1, u_lora.shape[2]), F32)],
        compiler_params=pltpu.CompilerParams(dimension_semantics=("parallel", "arbitrary"),
                                             vmem_limit_bytes=VMEM_LIMIT),
        name="rwkv7",
    )(u_rkv, u_lora, *params)


def _post_kernel(x_ref, osb_ref, orw_ref, gate_ref, p_ref, wupsb_ref, wuprw_ref, wout_ref,
                 gmlp_ref, wff1_ref, wff2_ref, gple_ref, wpg_ref, wpp_ref, gfin_ref, o_ref,
                 *, ff_chunk, final):
    x = x_ref[...]
    d = x.shape[1]
    merged = (gate_ref[:, :d].astype(F32) * _dot(osb_ref[...], wupsb_ref[...])
              + gate_ref[:, d:].astype(F32) * _dot(orw_ref[...], wuprw_ref[...]))
    x = x + _dot(merged.astype(BF16), wout_ref[...])

    h = _rms(x, gmlp_ref[...]).astype(BF16)
    mlp = jnp.zeros_like(x)
    for f0 in range(0, wff1_ref.shape[1], ff_chunk):
        hid = jnp.maximum(_dot(h, wff1_ref[:, f0:f0 + ff_chunk]), 0.0)
        mlp = mlp + _dot((hid * hid).astype(BF16), wff2_ref[f0:f0 + ff_chunk, :])
    x = x + mlp

    g = jax.nn.sigmoid(_dot(_rms(x, gple_ref[...]).astype(BF16), wpg_ref[...]))
    x = x + g * _dot(p_ref[...].astype(BF16), wpp_ref[...])
    o_ref[...] = _rms(x, gfin_ref[...]) if final else x


def _post(x2, o_sb, o_rw, gates, p2, weights, tm, final):
    t, d = x2.shape
    row = lambda n: pl.BlockSpec((tm, n), lambda i: (i, 0))
    kernel = functools.partial(_post_kernel, ff_chunk=1024, final=final)
    return pl.pallas_call(
        kernel,
        grid=(t // tm,),
        in_specs=[row(d), row(o_sb.shape[1]), row(o_rw.shape[1]), row(gates.shape[1]), row(p2.shape[1])]
                 + [_resident(w.shape) for w in weights],
        out_specs=row(d),
        out_shape=jax.ShapeDtypeStruct((t, d), F32),
        compiler_params=pltpu.CompilerParams(dimension_semantics=("parallel",),
                                             vmem_limit_bytes=VMEM_LIMIT),
        name="post",
    )(x2, o_sb, o_rw, gates, p2, *weights)


def kernel(x, p, attn_norm_g, w_in, shift_mu, decay_w0, decay_w2, iclr_a0, iclr_a2, gate_g2, k_k, k_a, r_k, ln_x_w, ln_x_b, w_up_sb, w_up_rw, w_out, mlp_norm_g, w_ff1, w_ff2, ple_norm_g, w_ple_gate, w_ple_proj, final_norm_g):
    bsz, seq, d = x.shape
    depth = w_in.shape[0]
    sb_width = w_up_sb.shape[1]
    rw_width = w_up_rw.shape[1]
    sb_cols = 3 * sb_width
    rkv_cols = 3 * rw_width
    lora_cols = DECAY_LORA + ICLR_LORA + GATE_LORA
    rw_end = sb_cols + rkv_cols + lora_cols
    tokens = bsz * seq
    tm_in = min(512, tokens)
    tm_post = min(512, tokens)
    vec = lambda a: a.reshape(1, -1).astype(F32)
    bf = lambda a: a.astype(BF16)

    x2 = x.reshape(tokens, d)
    for i in range(depth):
        wi = w_in[i]
        u_sb, u_rkv, u_lora, gates = _inproj(
            x2, vec(attn_norm_g[i]), bf(wi[:, :sb_cols]), bf(wi[:, sb_cols:sb_cols + rkv_cols]),
            bf(wi[:, sb_cols + rkv_cols:rw_end]), bf(wi[:, rw_end:]), tm_in)

        o_sb = _sb_attention(u_sb.reshape(bsz, seq, sb_cols), sb_width, min(SB_TILE, seq))

        mu = shift_mu[i]
        w2_pad = jnp.concatenate([decay_w2[i], jnp.zeros((ICLR_LORA, rw_width), F32)], axis=0)
        a2_pad = jnp.concatenate([jnp.zeros((DECAY_LORA, rw_width), F32), iclr_a2[i]], axis=0)
        rw_params = (
            vec(mu[:rkv_cols]), vec(mu[rkv_cols:]), vec(decay_w0[i]), bf(w2_pad), vec(iclr_a0[i]),
            bf(a2_pad), bf(gate_g2[i]), vec(k_k[i]), vec(k_a[i]), vec(r_k[i]), vec(ln_x_w[i]),
            vec(ln_x_b[i]))
        o_rw = _rwkv(u_rkv.reshape(bsz, seq, rkv_cols), u_lora.reshape(bsz, seq, lora_cols),
                     rw_params, rw_width, math.gcd(bsz, RW_SEQS))

        post_weights = (bf(w_up_sb[i]), bf(w_up_rw[i]), bf(w_out[i]), vec(mlp_norm_g[i]),
                        bf(w_ff1[i]), bf(w_ff2[i]), vec(ple_norm_g[i]), bf(w_ple_gate[i]),
                        bf(w_ple_proj[i]), vec(final_norm_g))
        x2 = _post(x2, o_sb.reshape(tokens, sb_width), o_rw.reshape(tokens, rw_width), gates,
                   p[i].reshape(tokens, -1), post_weights, tm_post, final=(i == depth - 1))
    return x2.reshape(bsz, seq, d)
```

```python
import functools
import math

import jax
import jax.numpy as jnp
from jax import lax
from jax.experimental import pallas as pl
from jax.experimental.pallas import tpu as pltpu

NORM_EPS = 1e-6
GN_EPS = 64e-5
HEAD_DIM = 64
LANES = 128
SB_TILE = 256
IN_ROWS = 512
POST_ROWS = 512
POST_FF_CHUNK = 1024
RW_SEQS = 16
RW_STREAM_SEQS = 4
RW_CHUNK = 64
DECAY_LORA = 64
ICLR_LORA = 64
GATE_LORA = 128
VMEM_LIMIT = 56 * 1024 * 1024

F32 = jnp.float32
BF16 = jnp.bfloat16

_NT = (((1,), (1,)), ((), ()))


def _dot(a, b):
    return jnp.dot(a, b, preferred_element_type=F32)


def _dot_nt(a, b):
    return lax.dot_general(a, b, _NT, preferred_element_type=F32)


def _bdot(a, b):
    return lax.dot_general(a, b, (((2,), (1,)), ((0,), (0,))), preferred_element_type=F32)


def _bdot_nt(a, b):
    return lax.dot_general(a, b, (((2,), (2,)), ((0,), (0,))), preferred_element_type=F32)


def _split2(x):
    hi = x.astype(BF16)
    lo = (x - hi.astype(F32)).astype(BF16)
    return hi, lo


def _rms(x, g):
    return x * lax.rsqrt(jnp.mean(x * x, axis=-1, keepdims=True) + NORM_EPS) * g


def _resident(shape):
    nd = len(shape)
    return pl.BlockSpec(shape, lambda *_: (0,) * nd, pipeline_mode=pl.Buffered(1))


def _inproj_kernel(x_ref, g_ref, wsb_ref, wrkv_ref, wlora_ref, wgate_ref,
                   sb_ref, rkv_ref, lora_ref, gate_ref):
    hb = _rms(x_ref[...], g_ref[...]).astype(BF16)
    gate_ref[...] = jax.nn.sigmoid(_dot(hb, wgate_ref[...])).astype(gate_ref.dtype)
    sb_ref[...] = _dot(hb, wsb_ref[...]).astype(sb_ref.dtype)
    rkv_ref[...] = _dot(hb, wrkv_ref[...])
    lora_ref[...] = _dot(hb, wlora_ref[...])


def _inproj(x2, g, wsb, wrkv, wlora, wgate, tm):
    t, d = x2.shape
    row = lambda n: pl.BlockSpec((tm, n), lambda i: (i, 0))
    ws = (wsb, wrkv, wlora, wgate)
    return pl.pallas_call(
        _inproj_kernel,
        grid=(t // tm,),
        in_specs=[row(d), _resident(g.shape)] + [_resident(w.shape) for w in ws],
        out_specs=[row(w.shape[1]) for w in ws],
        out_shape=[jax.ShapeDtypeStruct((t, wsb.shape[1]), BF16),
                   jax.ShapeDtypeStruct((t, wrkv.shape[1]), F32),
                   jax.ShapeDtypeStruct((t, wlora.shape[1]), F32),
                   jax.ShapeDtypeStruct((t, wgate.shape[1]), BF16)],
        compiler_params=pltpu.CompilerParams(dimension_semantics=("parallel",),
                                             vmem_limit_bytes=VMEM_LIMIT),
        name="inproj",
    )(x2, g, *ws)


LOG2_E = math.log2(math.e)
SOFTPLUS_LINEAR = 30.0
MASKED = -1e30


def _sb_attn_kernel(q_ref, k_ref, v_ref, o_ref, hl_ref, z_ref, bias_ref, *, scale, tile):
    heads = LANES // HEAD_DIM
    rows = heads * tile
    n_tiles = q_ref.shape[1] // tile
    lane = lax.broadcasted_iota(jnp.int32, (tile, LANES), 1)
    row = lax.broadcasted_iota(jnp.int32, (rows, tile), 0) % tile
    col = lax.broadcasted_iota(jnp.int32, (rows, tile), 1)
    bias_ref[...] = jnp.where(col < row, 0.0, MASKED)
    key_j = lax.broadcasted_iota(jnp.int32, (2 * tile, tile), 0) % tile
    key_s = lax.broadcasted_iota(jnp.int32, (2 * tile, tile), 1)
    from_key = jnp.where(key_j >= key_s, -1.0, 0.0).astype(BF16)

    def tile_rows(t):
        return pl.ds(t * tile, tile)

    @functools.cache
    def head_queries(qt):
        q = q_ref[0, tile_rows(qt), :] * scale
        zero = jnp.zeros_like(q)
        return jnp.concatenate([jnp.where(lane // HEAD_DIM == h, q, zero) for h in range(heads)], axis=0)

    def store_scores(qt, kt, slot):
        z = _dot_nt(head_queries(qt), k_ref[0, tile_rows(kt), :])
        z_ref[slot] = z + bias_ref[...] if qt == kt else z

    def split_scores(slot):
        z = z_ref[slot]
        hi, lo = _split2(jnp.where(z > SOFTPLUS_LINEAR, z, jnp.log(1.0 + jnp.exp2(z * LOG2_E))))
        hl_ref[slot, :, :tile] = hi
        hl_ref[slot, :, tile:] = lo

    def weights(qt, kt, slot, c, acc):
        after = _dot(hl_ref[slot], from_key)
        if qt == kt:
            w = jnp.exp(after + z_ref[slot])
            acc = _dot(w.astype(BF16), v_ref[0, tile_rows(kt), :])
            c = after[:, 0:1]
        else:
            w = jnp.exp(after + z_ref[slot] + c)
            acc = acc + _dot(w.astype(BF16), v_ref[0, tile_rows(kt), :])
            c = c + after[:, 0:1]
        if kt == 0:
            out = acc[:tile]
            for h in range(1, heads):
                out = jnp.where(lane // HEAD_DIM == h, acc[h * tile:(h + 1) * tile], out)
            o_ref[0, tile_rows(qt), :] = out.astype(o_ref.dtype)
        return c, acc

    pairs = [(qt, kt) for qt in range(n_tiles) for kt in range(qt, -1, -1)]
    store_scores(*pairs[0], 0)
    split_scores(0)
    if len(pairs) > 1:
        store_scores(*pairs[1], 1)
    c = acc = None
    for n, pair in enumerate(pairs):
        if n + 1 < len(pairs):
            split_scores((n + 1) % 3)
        if n + 2 < len(pairs):
            store_scores(*pairs[n + 2], (n + 2) % 3)
        c, acc = weights(*pair, n % 3, c, acc)


def _sb_attention(u_sb, width, tile):
    b, s, _ = u_sb.shape
    blocks = width // LANES
    rows = (LANES // HEAD_DIM) * tile
    kernel = functools.partial(_sb_attn_kernel, scale=1.0 / math.sqrt(HEAD_DIM), tile=tile)
    seq_block = lambda col0: pl.BlockSpec((1, s, LANES), lambda bi, hp: (bi, 0, col0 + hp))
    return pl.pallas_call(
        kernel,
        grid=(b, blocks),
        in_specs=[seq_block(0), seq_block(blocks), seq_block(2 * blocks)],
        out_specs=seq_block(0),
        out_shape=jax.ShapeDtypeStruct((b, s, width), BF16),
        scratch_shapes=[pltpu.VMEM((3, rows, 2 * tile), BF16),
                        pltpu.VMEM((3, rows, tile), F32),
                        pltpu.VMEM((rows, tile), F32)],
        compiler_params=pltpu.CompilerParams(dimension_semantics=("parallel", "parallel"),
                                             vmem_limit_bytes=VMEM_LIMIT),
        name="sb_attention",
    )(u_sb, u_sb, u_sb)


def _advance(gen):
    return next(gen, None)


def _rwkv_kernel(rkv_ref, lora_ref, *refs):
    params, (o_ref, state_ref, prev_rkv_ref, prev_lora_ref) = refs[:-4], refs[-4:]
    pairs = o_ref.shape[2] // LANES

    @pl.when(pl.program_id(1) == 0)
    def _():
        state_ref[...] = jnp.zeros_like(state_ref)
        prev_rkv_ref[...] = jnp.zeros_like(prev_rkv_ref)
        prev_lora_ref[...] = jnp.zeros_like(prev_lora_ref)

    nseq = rkv_ref.shape[0]
    per = math.gcd(nseq, RW_STREAM_SEQS)
    streams = []
    for s0 in range(0, nseq, per):
        seqs = slice(s0, s0 + per)
        groups = slice(s0 * pairs, (s0 + per) * pairs)
        streams.append(_rwkv_stream(rkv_ref.at[seqs], lora_ref.at[seqs], *params, o_ref.at[seqs],
                                    state_ref.at[groups], prev_rkv_ref.at[seqs], prev_lora_ref.at[seqs]))
    while _advance(streams[0]) != "front end done":
        pass
    for i, current in enumerate(streams):
        following = streams[i + 1] if i + 1 < len(streams) else None
        running, front = True, following is not None
        while running or front:
            if running:
                running = _advance(current) is not None
            if front:
                front = _advance(following) != "front end done"


def _rwkv_stream(rkv_ref, lora_ref, mu_rkv_ref, mu_lora_ref, w0_ref, w2_ref, a0_ref, a2_ref,
                 g2_ref, kk_ref, ka_ref, rk_ref, lnw_ref, lnb_ref, o_ref,
                 state_ref, prev_rkv_ref, prev_lora_ref):
    nseq, c_len = rkv_ref.shape[0], rkv_ref.shape[1]
    width = o_ref.shape[2]
    pairs = width // LANES
    rows = nseq * c_len

    def token_shift(u_ref, prev_ref, mu_ref):
        u = u_ref[...].reshape(rows, u_ref.shape[2])
        t = lax.broadcasted_iota(jnp.int32, u.shape, 0)
        prev = pltpu.roll(u, 1, 0)
        for s in range(nseq):
            prev = jnp.where(t == s * c_len, prev_ref[s], prev)
            prev_ref[s] = u[(s + 1) * c_len - 1:(s + 1) * c_len, :]
        return u + (prev - u) * mu_ref[...]

    rkv = token_shift(rkv_ref, prev_rkv_ref, mu_rkv_ref)
    yield "front end"
    lora = token_shift(lora_ref, prev_lora_ref, mu_lora_ref)
    r = rkv[:, :width]
    k = rkv[:, width:2 * width]
    v = rkv[:, 2 * width:]
    x_wa = lora[:, :DECAY_LORA + ICLR_LORA]
    x_g = lora[:, DECAY_LORA + ICLR_LORA:]

    d_w = w0_ref[...] + _dot(jnp.tanh(x_wa).astype(BF16), w2_ref[...])
    log_decay = jax.nn.sigmoid(d_w) * -math.exp(-0.5)
    iclr = jax.nn.sigmoid(a0_ref[...] + _dot(x_wa.astype(BF16), a2_ref[...]))
    gate = _dot(jax.nn.sigmoid(x_g).astype(BF16), g2_ref[...])
    yield "front end"

    lane_i = lax.broadcasted_iota(jnp.int32, (LANES, LANES), 0)
    lane_j = lax.broadcasted_iota(jnp.int32, (LANES, LANES), 1)
    same_head = lane_i // HEAD_DIM == lane_j // HEAD_DIM

    def head_sum(x):
        first_head = lax.broadcasted_iota(jnp.int32, x.shape, 2) < HEAD_DIM
        return jnp.where(first_head,
                         jnp.sum(jnp.where(first_head, x, 0.0), axis=-1, keepdims=True),
                         jnp.sum(jnp.where(first_head, 0.0, x), axis=-1, keepdims=True))

    def group(x):
        return jnp.stack([x[s * c_len:(s + 1) * c_len, p * LANES:(p + 1) * LANES]
                          for s in range(nseq) for p in range(pairs)])

    def group_param(ref):
        return jnp.stack([ref[:, p * LANES:(p + 1) * LANES] for _ in range(nseq) for p in range(pairs)])

    k_eff = group(k * (1.0 + (iclr - 1.0) * ka_ref[...]))
    kk = group(k * kk_ref[...])
    kk = kk * lax.rsqrt(jnp.maximum(head_sum(kk * kk), 1e-24))
    yield "front end"

    t_i = lax.broadcasted_iota(jnp.int32, (rows, rows), 0)
    t_j = lax.broadcasted_iota(jnp.int32, (rows, rows), 1)
    tri = ((t_j <= t_i) & (t_i // c_len == t_j // c_len)).astype(BF16)
    cum_flat = sum(_dot(tri, part) for part in _split2(log_decay))
    cum = group(cum_flat)
    w_prev = group(jnp.exp(cum_flat - log_decay))
    yield "front end"
    w_t = jnp.exp(cum)
    inv_w = jnp.exp(-cum)
    w_last = w_t[:, c_len - 1:c_len, :]
    r = group(r)
    v = group(v)
    iclr = group(iclr)
    r_t = r * w_t
    a_t = -kk * w_prev
    b_t = kk * iclr * inv_w
    k_t = k_eff * inv_w
    yield "front end done"

    row = lax.broadcasted_iota(jnp.int32, (c_len, LANES), 0)
    col = lax.broadcasted_iota(jnp.int32, (c_len, LANES), 1) % HEAD_DIM
    strict = col < row
    incl = col <= row
    eye = (col == row).astype(F32)

    def bd(x):
        m = lax.broadcasted_iota(jnp.int32, x.shape, 2) < HEAD_DIM
        zero = jnp.zeros_like(x)
        return jnp.concatenate([jnp.where(m, x, zero), jnp.where(m, zero, x)], axis=1)

    lhs = jnp.concatenate([a_t, r_t], axis=1).astype(BF16)
    rhs = jnp.concatenate([bd(b_t.astype(BF16)), bd(k_t.astype(BF16))], axis=1)
    a_all = _bdot_nt(lhs, rhs)
    a_ab = jnp.where(strict, a_all[:, :c_len, :LANES], 0.0)
    a_ak = jnp.where(strict, a_all[:, :c_len, LANES:], 0.0)
    a_rb = jnp.where(incl, a_all[:, c_len:, :LANES], 0.0)
    a_rk = jnp.where(incl, a_all[:, c_len:, LANES:], 0.0)
    yield "chain"

    t_inv = eye + a_ab
    power = a_ab.astype(BF16)
    power = _bdot(power, bd(power))
    yield "chain"
    n = 4
    while n < c_len:
        both = _bdot(jnp.concatenate([t_inv, power], axis=1).astype(BF16), bd(power.astype(BF16)))
        t_inv = t_inv + both[:, :c_len]
        power = both[:, c_len:]
        yield "chain"
        n *= 2
    t_inv = t_inv + _bdot(t_inv.astype(BF16), bd(power.astype(BF16)))
    yield "chain"

    state = state_ref[...]
    bd_v = bd(v.astype(BF16))
    from_state = _bdot_nt(lhs, state.astype(BF16))
    yield "chain"
    u = _bdot(t_inv.astype(BF16),
              bd((from_state[:, :c_len] + _bdot(a_ak.astype(BF16), bd_v)).astype(BF16)))
    yield "chain"
    y = from_state[:, c_len:] + _bdot(
        jnp.concatenate([a_rb, a_rk], axis=2).astype(BF16),
        jnp.concatenate([bd(u.astype(BF16)), bd_v], axis=1))
    yield "chain"
    uv = jnp.concatenate([u, v], axis=1)
    bk = jnp.concatenate([b_t * w_last, k_t * w_last], axis=1)
    grown = _bdot(jnp.swapaxes(uv, 1, 2).astype(BF16), bk.astype(BF16))
    state_ref[...] = state * w_last + jnp.where(same_head, grown, 0.0)
    yield "chain"

    mean = head_sum(y) * (1.0 / HEAD_DIM)
    d = y - mean
    var = head_sum(d * d) * (1.0 / HEAD_DIM)
    y = d * lax.rsqrt(var + GN_EPS) * group_param(lnw_ref) + group_param(lnb_ref)
    y = y + head_sum(r * k_eff * group_param(rk_ref)) * v
    out = (y * group(gate)).astype(o_ref.dtype)
    for s in range(nseq):
        for p in range(pairs):
            o_ref[s, :, p * LANES:(p + 1) * LANES] = out[s * pairs + p]


def _rwkv(u_rkv, u_lora, params, width, nseq):
    b, s, _ = u_rkv.shape
    c_len = RW_CHUNK
    chunk = lambda n: pl.BlockSpec((nseq, c_len, n), lambda bi, ci: (bi, ci, 0))
    return pl.pallas_call(
        _rwkv_kernel,
        grid=(b // nseq, s // c_len),
        in_specs=[chunk(u_rkv.shape[2]), chunk(u_lora.shape[2])] + [_resident(a.shape) for a in params],
        out_specs=chunk(width),
        out_shape=jax.ShapeDtypeStruct((b, s, width), BF16),
        scratch_shapes=[pltpu.VMEM((nseq * (width // LANES), LANES, LANES), F32),
                        pltpu.VMEM((nseq, 1, u_rkv.shape[2]), F32),
                        pltpu.VMEM((nseq, 1, u_lora.shape[2]), F32)],
        compiler_params=pltpu.CompilerParams(dimension_semantics=("parallel", "arbitrary"),
                                             vmem_limit_bytes=VMEM_LIMIT),
        name="rwkv7",
    )(u_rkv, u_lora, *params)


def _post_kernel(x_ref, osb_ref, orw_ref, gate_ref, p_ref, wupsb_ref, wuprw_ref, wout_ref,
                 gmlp_ref, wff1_ref, wff2_ref, gple_ref, wpg_ref, wpp_ref, gfin_ref, o_ref,
                 *, ff_chunk, final):
    x = x_ref[...]
    d = x.shape[1]
    merged = (gate_ref[:, :d].astype(F32) * _dot(osb_ref[...], wupsb_ref[...])
              + gate_ref[:, d:].astype(F32) * _dot(orw_ref[...], wuprw_ref[...]))
    x = x + _dot(merged.astype(BF16), wout_ref[...])

    h = _rms(x, gmlp_ref[...]).astype(BF16)
    mlp = jnp.zeros_like(x)
    for f0 in range(0, wff1_ref.shape[1], ff_chunk):
        hid = jnp.maximum(_dot(h, wff1_ref[:, f0:f0 + ff_chunk]), 0.0)
        mlp = mlp + _dot((hid * hid).astype(BF16), wff2_ref[f0:f0 + ff_chunk, :])
    x = x + mlp

    g = jax.nn.sigmoid(_dot(_rms(x, gple_ref[...]).astype(BF16), wpg_ref[...]))
    x = x + g * _dot(p_ref[...].astype(BF16), wpp_ref[...])
    o_ref[...] = _rms(x, gfin_ref[...]) if final else x


def _post(x2, o_sb, o_rw, gates, p2, weights, tm, final):
    t, d = x2.shape
    row = lambda n: pl.BlockSpec((tm, n), lambda i: (i, 0))
    kernel = functools.partial(_post_kernel, ff_chunk=POST_FF_CHUNK, final=final)
    return pl.pallas_call(
        kernel,
        grid=(t // tm,),
        in_specs=[row(d), row(o_sb.shape[1]), row(o_rw.shape[1]), row(gates.shape[1]), row(p2.shape[1])]
                 + [_resident(w.shape) for w in weights],
        out_specs=row(d),
        out_shape=jax.ShapeDtypeStruct((t, d), F32),
        compiler_params=pltpu.CompilerParams(dimension_semantics=("parallel",),
                                             vmem_limit_bytes=VMEM_LIMIT),
        name="post",
    )(x2, o_sb, o_rw, gates, p2, *weights)


def kernel(x, p, attn_norm_g, w_in, shift_mu, decay_w0, decay_w2, iclr_a0, iclr_a2, gate_g2, k_k, k_a, r_k, ln_x_w, ln_x_b, w_up_sb, w_up_rw, w_out, mlp_norm_g, w_ff1, w_ff2, ple_norm_g, w_ple_gate, w_ple_proj, final_norm_g):
    bsz, seq, d = x.shape
    depth = w_in.shape[0]
    sb_width = w_up_sb.shape[1]
    rw_width = w_up_rw.shape[1]
    sb_cols = 3 * sb_width
    rkv_cols = 3 * rw_width
    lora_cols = DECAY_LORA + ICLR_LORA + GATE_LORA
    rw_end = sb_cols + rkv_cols + lora_cols
    tokens = bsz * seq
    tm_in = min(IN_ROWS, tokens)
    tm_post = min(POST_ROWS, tokens)
    vec = lambda a: a.reshape(1, -1).astype(F32)
    bf = lambda a: a.astype(BF16)

    x2 = x.reshape(tokens, d)
    for i in range(depth):
        wi = w_in[i]
        u_sb, u_rkv, u_lora, gates = _inproj(
            x2, vec(attn_norm_g[i]), bf(wi[:, :sb_cols]), bf(wi[:, sb_cols:sb_cols + rkv_cols]),
            bf(wi[:, sb_cols + rkv_cols:rw_end]), bf(wi[:, rw_end:]), tm_in)

        o_sb = _sb_attention(u_sb.reshape(bsz, seq, sb_cols), sb_width, min(SB_TILE, seq))

        mu = shift_mu[i]
        w2_pad = jnp.concatenate([decay_w2[i], jnp.zeros((ICLR_LORA, rw_width), F32)], axis=0)
        a2_pad = jnp.concatenate([jnp.zeros((DECAY_LORA, rw_width), F32), iclr_a2[i]], axis=0)
        rw_params = (
            vec(mu[:rkv_cols]), vec(mu[rkv_cols:]), vec(decay_w0[i]), bf(w2_pad), vec(iclr_a0[i]),
            bf(a2_pad), bf(gate_g2[i]), vec(k_k[i]), vec(k_a[i]), vec(r_k[i]), vec(ln_x_w[i]),
            vec(ln_x_b[i]))
        o_rw = _rwkv(u_rkv.reshape(bsz, seq, rkv_cols), u_lora.reshape(bsz, seq, lora_cols),
                     rw_params, rw_width, math.gcd(bsz, RW_SEQS))

        post_weights = (bf(w_up_sb[i]), bf(w_up_rw[i]), bf(w_out[i]), vec(mlp_norm_g[i]),
                        bf(w_ff1[i]), bf(w_ff2[i]), vec(ple_norm_g[i]), bf(w_ple_gate[i]),
                        bf(w_ple_proj[i]), vec(final_norm_g))
        x2 = _post(x2, o_sb.reshape(tokens, sb_width), o_rw.reshape(tokens, rw_width), gates,
                   p[i].reshape(tokens, -1), post_weights, tm_post, final=(i == depth - 1))
    return x2.reshape(bsz, seq, d)
```

```python
import functools
import math

import jax
import jax.numpy as jnp
from jax import lax
from jax.experimental import pallas as pl
from jax.experimental.pallas import tpu as pltpu

NORM_EPS = 1e-6
GN_EPS = 64e-5
HEAD_DIM = 64
LANES = 128
SB_TILE = 256
IN_ROWS = 512
POST_ROWS = 512
POST_FF_CHUNK = 1024
RW_SEQS = 16
RW_STREAM_SEQS = 4
RW_CHUNK = 64
DECAY_LORA = 64
ICLR_LORA = 64
GATE_LORA = 128
VMEM_LIMIT = 56 * 1024 * 1024

F32 = jnp.float32
BF16 = jnp.bfloat16

_NT = (((1,), (1,)), ((), ()))


def _dot(a, b):
    return jnp.dot(a, b, preferred_element_type=F32)


def _dot_nt(a, b):
    return lax.dot_general(a, b, _NT, preferred_element_type=F32)


def _bdot(a, b):
    return lax.dot_general(a, b, (((2,), (1,)), ((0,), (0,))), preferred_element_type=F32)


def _bdot_nt(a, b):
    return lax.dot_general(a, b, (((2,), (2,)), ((0,), (0,))), preferred_element_type=F32)


def _split2(x):
    hi = x.astype(BF16)
    lo = (x - hi.astype(F32)).astype(BF16)
    return hi, lo


def _rms(x, g):
    return x * lax.rsqrt(jnp.mean(x * x, axis=-1, keepdims=True) + NORM_EPS) * g


def _resident(shape):
    nd = len(shape)
    return pl.BlockSpec(shape, lambda *_: (0,) * nd, pipeline_mode=pl.Buffered(1))


def _inproj_kernel(x_ref, g_ref, wsb_ref, wrkv_ref, wlora_ref, wgate_ref,
                   sb_ref, rkv_ref, lora_ref, gate_ref):
    hb = _rms(x_ref[...], g_ref[...]).astype(BF16)
    gate_ref[...] = jax.nn.sigmoid(_dot(hb, wgate_ref[...])).astype(gate_ref.dtype)
    sb_ref[...] = _dot(hb, wsb_ref[...]).astype(sb_ref.dtype)
    rkv_ref[...] = _dot(hb, wrkv_ref[...])
    lora_ref[...] = _dot(hb, wlora_ref[...])


def _inproj(x2, g, wsb, wrkv, wlora, wgate, tm):
    t, d = x2.shape
    row = lambda n: pl.BlockSpec((tm, n), lambda i: (i, 0))
    ws = (wsb, wrkv, wlora, wgate)
    return pl.pallas_call(
        _inproj_kernel,
        grid=(t // tm,),
        in_specs=[row(d), _resident(g.shape)] + [_resident(w.shape) for w in ws],
        out_specs=[row(w.shape[1]) for w in ws],
        out_shape=[jax.ShapeDtypeStruct((t, wsb.shape[1]), BF16),
                   jax.ShapeDtypeStruct((t, wrkv.shape[1]), F32),
                   jax.ShapeDtypeStruct((t, wlora.shape[1]), F32),
                   jax.ShapeDtypeStruct((t, wgate.shape[1]), BF16)],
        compiler_params=pltpu.CompilerParams(dimension_semantics=("parallel",),
                                             vmem_limit_bytes=VMEM_LIMIT),
        name="inproj",
    )(x2, g, *ws)


LOG2_E = math.log2(math.e)
SOFTPLUS_LINEAR = 30.0
MASKED = -1e30


def _sb_attn_kernel(q_ref, k_ref, v_ref, o_ref, hl_ref, z_ref, bias_ref, *, scale, tile):
    heads = LANES // HEAD_DIM
    rows = heads * tile
    n_tiles = q_ref.shape[1] // tile
    lane = lax.broadcasted_iota(jnp.int32, (tile, LANES), 1)
    row = lax.broadcasted_iota(jnp.int32, (rows, tile), 0) % tile
    col = lax.broadcasted_iota(jnp.int32, (rows, tile), 1)
    bias_ref[...] = jnp.where(col < row, 0.0, MASKED)
    key_j = lax.broadcasted_iota(jnp.int32, (2 * tile, tile), 0) % tile
    key_s = lax.broadcasted_iota(jnp.int32, (2 * tile, tile), 1)
    from_key = jnp.where(key_j >= key_s, -1.0, 0.0).astype(BF16)

    def tile_rows(t):
        return pl.ds(t * tile, tile)

    @functools.cache
    def head_queries(qt):
        q = q_ref[0, tile_rows(qt), :] * scale
        zero = jnp.zeros_like(q)
        return jnp.concatenate([jnp.where(lane // HEAD_DIM == h, q, zero) for h in range(heads)], axis=0)

    def store_scores(qt, kt, slot):
        z = _dot_nt(head_queries(qt), k_ref[0, tile_rows(kt), :])
        z_ref[slot] = z + bias_ref[...] if qt == kt else z

    def split_scores(slot):
        z = z_ref[slot]
        hi, lo = _split2(jnp.where(z > SOFTPLUS_LINEAR, z, jnp.log(1.0 + jnp.exp2(z * LOG2_E))))
        hl_ref[slot, :, :tile] = hi
        hl_ref[slot, :, tile:] = lo

    def weights(qt, kt, slot, c, acc):
        after = _dot(hl_ref[slot], from_key)
        if qt == kt:
            w = jnp.exp(after + z_ref[slot])
            acc = _dot(w.astype(BF16), v_ref[0, tile_rows(kt), :])
            c = after[:, 0:1]
        else:
            w = jnp.exp(after + z_ref[slot] + c)
            acc = acc + _dot(w.astype(BF16), v_ref[0, tile_rows(kt), :])
            c = c + after[:, 0:1]
        if kt == 0:
            out = acc[:tile]
            for h in range(1, heads):
                out = jnp.where(lane // HEAD_DIM == h, acc[h * tile:(h + 1) * tile], out)
            o_ref[0, tile_rows(qt), :] = out.astype(o_ref.dtype)
        return c, acc

    pairs = [(qt, kt) for qt in range(n_tiles) for kt in range(qt, -1, -1)]
    store_scores(*pairs[0], 0)
    split_scores(0)
    if len(pairs) > 1:
        store_scores(*pairs[1], 1)
    c = acc = None
    for n, pair in enumerate(pairs):
        if n + 1 < len(pairs):
            split_scores((n + 1) % 3)
        if n + 2 < len(pairs):
            store_scores(*pairs[n + 2], (n + 2) % 3)
        c, acc = weights(*pair, n % 3, c, acc)


def _sb_attention(u_sb, width, tile):
    b, s, _ = u_sb.shape
    blocks = width // LANES
    rows = (LANES // HEAD_DIM) * tile
    kernel = functools.partial(_sb_attn_kernel, scale=1.0 / math.sqrt(HEAD_DIM), tile=tile)
    seq_block = lambda col0: pl.BlockSpec((1, s, LANES), lambda bi, hp: (bi, 0, col0 + hp))
    return pl.pallas_call(
        kernel,
        grid=(b, blocks),
        in_specs=[seq_block(0), seq_block(blocks), seq_block(2 * blocks)],
        out_specs=seq_block(0),
        out_shape=jax.ShapeDtypeStruct((b, s, width), BF16),
        scratch_shapes=[pltpu.VMEM((3, rows, 2 * tile), BF16),
                        pltpu.VMEM((3, rows, tile), F32),
                        pltpu.VMEM((rows, tile), F32)],
        compiler_params=pltpu.CompilerParams(dimension_semantics=("parallel", "parallel"),
                                             vmem_limit_bytes=VMEM_LIMIT),
        name="sb_attention",
    )(u_sb, u_sb, u_sb)


def _advance(gen):
    return next(gen, None)


def _rwkv_kernel(rkv_ref, lora_ref, *refs):
    params, (o_ref, state_ref, prev_rkv_ref, prev_lora_ref) = refs[:-4], refs[-4:]
    pairs = o_ref.shape[2] // LANES

    @pl.when(pl.program_id(1) == 0)
    def _():
        state_ref[...] = jnp.zeros_like(state_ref)
        prev_rkv_ref[...] = jnp.zeros_like(prev_rkv_ref)
        prev_lora_ref[...] = jnp.zeros_like(prev_lora_ref)

    nseq = rkv_ref.shape[0]
    per = math.gcd(nseq, RW_STREAM_SEQS)
    streams = []
    for s0 in range(0, nseq, per):
        seqs = slice(s0, s0 + per)
        groups = slice(s0 * pairs, (s0 + per) * pairs)
        streams.append(_rwkv_stream(rkv_ref.at[seqs], lora_ref.at[seqs], *params, o_ref.at[seqs],
                                    state_ref.at[groups], prev_rkv_ref.at[seqs], prev_lora_ref.at[seqs]))
    while _advance(streams[0]) != "front end done":
        pass
    for i, current in enumerate(streams):
        following = streams[i + 1] if i + 1 < len(streams) else None
        running, front = True, following is not None
        while running or front:
            if running:
                running = _advance(current) is not None
            if front:
                front = _advance(following) != "front end done"


def _rwkv_stream(rkv_ref, lora_ref, mu_rkv_ref, mu_lora_ref, w0_ref, w2_ref, a0_ref, a2_ref,
                 g2_ref, kk_ref, ka_ref, rk_ref, lnw_ref, lnb_ref, o_ref,
                 state_ref, prev_rkv_ref, prev_lora_ref):
    nseq, c_len = rkv_ref.shape[0], rkv_ref.shape[1]
    width = o_ref.shape[2]
    pairs = width // LANES
    rows = nseq * c_len

    def token_shift(u_ref, prev_ref, mu_ref):
        u = u_ref[...].reshape(rows, u_ref.shape[2])
        t = lax.broadcasted_iota(jnp.int32, u.shape, 0)
        prev = pltpu.roll(u, 1, 0)
        for s in range(nseq):
            prev = jnp.where(t == s * c_len, prev_ref[s], prev)
            prev_ref[s] = u[(s + 1) * c_len - 1:(s + 1) * c_len, :]
        return u + (prev - u) * mu_ref[...]

    rkv = token_shift(rkv_ref, prev_rkv_ref, mu_rkv_ref)
    yield "front end"
    lora = token_shift(lora_ref, prev_lora_ref, mu_lora_ref)
    r = rkv[:, :width]
    k = rkv[:, width:2 * width]
    v = rkv[:, 2 * width:]
    x_wa = lora[:, :DECAY_LORA + ICLR_LORA]
    x_g = lora[:, DECAY_LORA + ICLR_LORA:]

    d_w = w0_ref[...] + _dot(jnp.tanh(x_wa).astype(BF16), w2_ref[...])
    log_decay = jax.nn.sigmoid(d_w) * -math.exp(-0.5)
    iclr = jax.nn.sigmoid(a0_ref[...] + _dot(x_wa.astype(BF16), a2_ref[...]))
    gate = _dot(jax.nn.sigmoid(x_g).astype(BF16), g2_ref[...])
    yield "front end"

    lane_i = lax.broadcasted_iota(jnp.int32, (LANES, LANES), 0)
    lane_j = lax.broadcasted_iota(jnp.int32, (LANES, LANES), 1)
    same_head = lane_i // HEAD_DIM == lane_j // HEAD_DIM
    head_ones = jnp.concatenate([same_head, same_head], axis=0).astype(BF16)

    def head_sum(x):
        return _dot(jnp.concatenate(_split2(x.reshape(-1, LANES)), axis=1), head_ones).reshape(x.shape)

    def group(x):
        return jnp.stack([x[s * c_len:(s + 1) * c_len, p * LANES:(p + 1) * LANES]
                          for s in range(nseq) for p in range(pairs)])

    def group_param(ref):
        return jnp.stack([ref[:, p * LANES:(p + 1) * LANES] for _ in range(nseq) for p in range(pairs)])

    k_eff = group(k * (1.0 + (iclr - 1.0) * ka_ref[...]))
    kk = group(k * kk_ref[...])
    kk = kk * lax.rsqrt(jnp.maximum(head_sum(kk * kk), 1e-24))
    yield "front end"

    t_i = lax.broadcasted_iota(jnp.int32, (rows, rows), 0)
    t_j = lax.broadcasted_iota(jnp.int32, (rows, rows), 1)
    tri = ((t_j <= t_i) & (t_i // c_len == t_j // c_len)).astype(BF16)
    cum_flat = sum(_dot(tri, part) for part in _split2(log_decay))
    cum = group(cum_flat)
    w_prev = group(jnp.exp(cum_flat - log_decay))
    yield "front end"
    w_t = jnp.exp(cum)
    inv_w = jnp.exp(-cum)
    w_last = w_t[:, c_len - 1:c_len, :]
    r = group(r)
    v = group(v)
    iclr = group(iclr)
    r_t = r * w_t
    a_t = -kk * w_prev
    b_t = kk * iclr * inv_w
    k_t = k_eff * inv_w
    yield "front end done"

    row = lax.broadcasted_iota(jnp.int32, (c_len, LANES), 0)
    col = lax.broadcasted_iota(jnp.int32, (c_len, LANES), 1) % HEAD_DIM
    strict = col < row
    incl = col <= row
    eye = (col == row).astype(F32)

    def bd(x):
        m = lax.broadcasted_iota(jnp.int32, x.shape, 2) < HEAD_DIM
        zero = jnp.zeros_like(x)
        return jnp.concatenate([jnp.where(m, x, zero), jnp.where(m, zero, x)], axis=1)

    lhs = jnp.concatenate([a_t, r_t], axis=1).astype(BF16)
    rhs = jnp.concatenate([bd(b_t.astype(BF16)), bd(k_t.astype(BF16))], axis=1)
    a_all = _bdot_nt(lhs, rhs)
    a_ab = jnp.where(strict, a_all[:, :c_len, :LANES], 0.0)
    a_ak = jnp.where(strict, a_all[:, :c_len, LANES:], 0.0)
    a_rb = jnp.where(incl, a_all[:, c_len:, :LANES], 0.0)
    a_rk = jnp.where(incl, a_all[:, c_len:, LANES:], 0.0)
    yield "chain"

    t_inv = eye + a_ab
    power = a_ab.astype(BF16)
    power = _bdot(power, bd(power))
    yield "chain"
    n = 4
    while n < c_len:
        both = _bdot(jnp.concatenate([t_inv, power], axis=1).astype(BF16), bd(power.astype(BF16)))
        t_inv = t_inv + both[:, :c_len]
        power = both[:, c_len:]
        yield "chain"
        n *= 2
    t_inv = t_inv + _bdot(t_inv.astype(BF16), bd(power.astype(BF16)))
    yield "chain"

    state = state_ref[...]
    bd_v = bd(v.astype(BF16))
    from_state = _bdot_nt(lhs, state.astype(BF16))
    yield "chain"
    u = _bdot(t_inv.astype(BF16),
              bd((from_state[:, :c_len] + _bdot(a_ak.astype(BF16), bd_v)).astype(BF16)))
    yield "chain"
    y = from_state[:, c_len:] + _bdot(
        jnp.concatenate([a_rb, a_rk], axis=2).astype(BF16),
        jnp.concatenate([bd(u.astype(BF16)), bd_v], axis=1))
    yield "chain"
    uv = jnp.concatenate([u, v], axis=1)
    bk = jnp.concatenate([b_t * w_last, k_t * w_last], axis=1)
    grown = _bdot(jnp.swapaxes(uv, 1, 2).astype(BF16), bk.astype(BF16))
    state_ref[...] = state * w_last + jnp.where(same_head, grown, 0.0)
    yield "chain"

    mean = head_sum(y) * (1.0 / HEAD_DIM)
    d = y - mean
    var = head_sum(d * d) * (1.0 / HEAD_DIM)
    y = d * lax.rsqrt(var + GN_EPS) * group_param(lnw_ref) + group_param(lnb_ref)
    y = y + head_sum(r * k_eff * group_param(rk_ref)) * v
    out = (y * group(gate)).astype(o_ref.dtype)
    for s in range(nseq):
        for p in range(pairs):
            o_ref[s, :, p * LANES:(p + 1) * LANES] = out[s * pairs + p]


def _rwkv(u_rkv, u_lora, params, width, nseq):
    b, s, _ = u_rkv.shape
    c_len = RW_CHUNK
    chunk = lambda n: pl.BlockSpec((nseq, c_len, n), lambda bi, ci: (bi, ci, 0))
    return pl.pallas_call(
        _rwkv_kernel,
        grid=(b // nseq, s // c_len),
        in_specs=[chunk(u_rkv.shape[2]), chunk(u_lora.shape[2])] + [_resident(a.shape) for a in params],
        out_specs=chunk(width),
        out_shape=jax.ShapeDtypeStruct((b, s, width), BF16),
        scratch_shapes=[pltpu.VMEM((nseq * (width // LANES), LANES, LANES), F32),
                        pltpu.VMEM((nseq, 1, u_rkv.shape[2]), F32),
                        pltpu.VMEM((nseq, 1, u_lora.shape[2]), F32)],
        compiler_params=pltpu.CompilerParams(dimension_semantics=("parallel", "arbitrary"),
                                             vmem_limit_bytes=VMEM_LIMIT),
        name="rwkv7",
    )(u_rkv, u_lora, *params)


def _post_kernel(x_ref, osb_ref, orw_ref, gate_ref, p_ref, wupsb_ref, wuprw_ref, wout_ref,
                 gmlp_ref, wff1_ref, wff2_ref, gple_ref, wpg_ref, wpp_ref, gfin_ref, o_ref,
                 *, ff_chunk, final):
    x = x_ref[...]
    d = x.shape[1]
    merged = (gate_ref[:, :d].astype(F32) * _dot(osb_ref[...], wupsb_ref[...])
              + gate_ref[:, d:].astype(F32) * _dot(orw_ref[...], wuprw_ref[...]))
    x = x + _dot(merged.astype(BF16), wout_ref[...])

    h = _rms(x, gmlp_ref[...]).astype(BF16)
    mlp = jnp.zeros_like(x)
    for f0 in range(0, wff1_ref.shape[1], ff_chunk):
        hid = jnp.maximum(_dot(h, wff1_ref[:, f0:f0 + ff_chunk]), 0.0)
        mlp = mlp + _dot((hid * hid).astype(BF16), wff2_ref[f0:f0 + ff_chunk, :])
    x = x + mlp

    g = jax.nn.sigmoid(_dot(_rms(x, gple_ref[...]).astype(BF16), wpg_ref[...]))
    x = x + g * _dot(p_ref[...].astype(BF16), wpp_ref[...])
    o_ref[...] = _rms(x, gfin_ref[...]) if final else x


def _post(x2, o_sb, o_rw, gates, p2, weights, tm, final):
    t, d = x2.shape
    row = lambda n: pl.BlockSpec((tm, n), lambda i: (i, 0))
    kernel = functools.partial(_post_kernel, ff_chunk=POST_FF_CHUNK, final=final)
    return pl.pallas_call(
        kernel,
        grid=(t // tm,),
        in_specs=[row(d), row(o_sb.shape[1]), row(o_rw.shape[1]), row(gates.shape[1]), row(p2.shape[1])]
                 + [_resident(w.shape) for w in weights],
        out_specs=row(d),
        out_shape=jax.ShapeDtypeStruct((t, d), F32),
        compiler_params=pltpu.CompilerParams(dimension_semantics=("parallel",),
                                             vmem_limit_bytes=VMEM_LIMIT),
        name="post",
    )(x2, o_sb, o_rw, gates, p2, *weights)


def kernel(x, p, attn_norm_g, w_in, shift_mu, decay_w0, decay_w2, iclr_a0, iclr_a2, gate_g2, k_k, k_a, r_k, ln_x_w, ln_x_b, w_up_sb, w_up_rw, w_out, mlp_norm_g, w_ff1, w_ff2, ple_norm_g, w_ple_gate, w_ple_proj, final_norm_g):
    bsz, seq, d = x.shape
    depth = w_in.shape[0]
    sb_width = w_up_sb.shape[1]
    rw_width = w_up_rw.shape[1]
    sb_cols = 3 * sb_width
    rkv_cols = 3 * rw_width
    lora_cols = DECAY_LORA + ICLR_LORA + GATE_LORA
    rw_end = sb_cols + rkv_cols + lora_cols
    tokens = bsz * seq
    tm_in = min(IN_ROWS, tokens)
    tm_post = min(POST_ROWS, tokens)
    vec = lambda a: a.reshape(1, -1).astype(F32)
    bf = lambda a: a.astype(BF16)

    x2 = x.reshape(tokens, d)
    for i in range(depth):
        wi = w_in[i]
        u_sb, u_rkv, u_lora, gates = _inproj(
            x2, vec(attn_norm_g[i]), bf(wi[:, :sb_cols]), bf(wi[:, sb_cols:sb_cols + rkv_cols]),
            bf(wi[:, sb_cols + rkv_cols:rw_end]), bf(wi[:, rw_end:]), tm_in)

        o_sb = _sb_attention(u_sb.reshape(bsz, seq, sb_cols), sb_width, min(SB_TILE, seq))

        mu = shift_mu[i]
        w2_pad = jnp.concatenate([decay_w2[i], jnp.zeros((ICLR_LORA, rw_width), F32)], axis=0)
        a2_pad = jnp.concatenate([jnp.zeros((DECAY_LORA, rw_width), F32), iclr_a2[i]], axis=0)
        rw_params = (
            vec(mu[:rkv_cols]), vec(mu[rkv_cols:]), vec(decay_w0[i]), bf(w2_pad), vec(iclr_a0[i]),
            bf(a2_pad), bf(gate_g2[i]), vec(k_k[i]), vec(k_a[i]), vec(r_k[i]), vec(ln_x_w[i]),
            vec(ln_x_b[i]))
        o_rw = _rwkv(u_rkv.reshape(bsz, seq, rkv_cols), u_lora.reshape(bsz, seq, lora_cols),
                     rw_params, rw_width, math.gcd(bsz, RW_SEQS))

        post_weights = (bf(w_up_sb[i]), bf(w_up_rw[i]), bf(w_out[i]), vec(mlp_norm_g[i]),
                        bf(w_ff1[i]), bf(w_ff2[i]), vec(ple_norm_g[i]), bf(w_ple_gate[i]),
                        bf(w_ple_proj[i]), vec(final_norm_g))
        x2 = _post(x2, o_sb.reshape(tokens, sb_width), o_rw.reshape(tokens, rw_width), gates,
                   p[i].reshape(tokens, -1), post_weights, tm_post, final=(i == depth - 1))
    return x2.reshape(bsz, seq, d)
```

```python
import functools
import math

import jax
import jax.numpy as jnp
from jax import lax
from jax.experimental import pallas as pl
from jax.experimental.pallas import tpu as pltpu

NORM_EPS = 1e-6
GN_EPS = 64e-5
HEAD_DIM = 64
LANES = 128
SB_TILE = 256
IN_ROWS = 512
IN_STREAMS = 2
POST_ROWS = 512
POST_FF_CHUNK = 1024
RW_SEQS = 16
RW_STREAM_SEQS = 4
RW_CHUNK = 64
DECAY_LORA = 64
ICLR_LORA = 64
GATE_LORA = 128
VMEM_LIMIT = 56 * 1024 * 1024

F32 = jnp.float32
BF16 = jnp.bfloat16

_NT = (((1,), (1,)), ((), ()))


def _dot(a, b):
    return jnp.dot(a, b, preferred_element_type=F32)


def _dot_nt(a, b):
    return lax.dot_general(a, b, _NT, preferred_element_type=F32)


def _bdot(a, b):
    return lax.dot_general(a, b, (((2,), (1,)), ((0,), (0,))), preferred_element_type=F32)


def _bdot_nt(a, b):
    return lax.dot_general(a, b, (((2,), (2,)), ((0,), (0,))), preferred_element_type=F32)


def _split2(x):
    hi = x.astype(BF16)
    lo = (x - hi.astype(F32)).astype(BF16)
    return hi, lo


def _rms(x, g):
    return x * lax.rsqrt(jnp.mean(x * x, axis=-1, keepdims=True) + NORM_EPS) * g


def _resident(shape):
    nd = len(shape)
    return pl.BlockSpec(shape, lambda *_: (0,) * nd, pipeline_mode=pl.Buffered(1))


def _inproj_kernel(*refs):
    rows = refs[0].shape[0] // IN_STREAMS
    waiting = [_inproj_rows(pl.ds(i * rows, rows), *refs) for i in range(IN_STREAMS)]
    live = []
    while waiting or live:
        if waiting:
            live.append(waiting.pop(0))
        live = [g for g in live if _advance(g) is not None]


def _inproj_rows(rows, x_ref, g_ref, wsb_ref, wrkv_ref, wlora_ref, wgate_ref,
                 sb_ref, rkv_ref, lora_ref, gate_ref):
    hb = _rms(x_ref[rows, :], g_ref[...]).astype(BF16)
    yield "norm"
    gate_ref[rows, :] = jax.nn.sigmoid(_dot(hb, wgate_ref[...])).astype(gate_ref.dtype)
    yield "gate"
    sb_ref[rows, :] = _dot(hb, wsb_ref[...]).astype(sb_ref.dtype)
    yield "attention columns"
    rkv_ref[rows, :] = _dot(hb, wrkv_ref[...])
    lora_ref[rows, :] = _dot(hb, wlora_ref[...])


def _inproj(x2, g, wsb, wrkv, wlora, wgate, tm):
    t, d = x2.shape
    row = lambda n: pl.BlockSpec((tm, n), lambda i: (i, 0))
    ws = (wsb, wrkv, wlora, wgate)
    return pl.pallas_call(
        _inproj_kernel,
        grid=(t // tm,),
        in_specs=[row(d), _resident(g.shape)] + [_resident(w.shape) for w in ws],
        out_specs=[row(w.shape[1]) for w in ws],
        out_shape=[jax.ShapeDtypeStruct((t, wsb.shape[1]), BF16),
                   jax.ShapeDtypeStruct((t, wrkv.shape[1]), F32),
                   jax.ShapeDtypeStruct((t, wlora.shape[1]), F32),
                   jax.ShapeDtypeStruct((t, wgate.shape[1]), BF16)],
        compiler_params=pltpu.CompilerParams(dimension_semantics=("parallel",),
                                             vmem_limit_bytes=VMEM_LIMIT),
        name="inproj",
    )(x2, g, *ws)


LOG2_E = math.log2(math.e)
SOFTPLUS_LINEAR = 30.0
MASKED = -1e30


def _sb_attn_kernel(q_ref, k_ref, v_ref, o_ref, hl_ref, z_ref, bias_ref, *, scale, tile):
    heads = LANES // HEAD_DIM
    rows = heads * tile
    n_tiles = q_ref.shape[1] // tile
    lane = lax.broadcasted_iota(jnp.int32, (tile, LANES), 1)
    row = lax.broadcasted_iota(jnp.int32, (rows, tile), 0) % tile
    col = lax.broadcasted_iota(jnp.int32, (rows, tile), 1)
    bias_ref[...] = jnp.where(col < row, 0.0, MASKED)
    key_j = lax.broadcasted_iota(jnp.int32, (2 * tile, tile), 0) % tile
    key_s = lax.broadcasted_iota(jnp.int32, (2 * tile, tile), 1)
    from_key = jnp.where(key_j >= key_s, -1.0, 0.0).astype(BF16)

    def tile_rows(t):
        return pl.ds(t * tile, tile)

    @functools.cache
    def head_queries(qt):
        q = q_ref[0, tile_rows(qt), :] * scale
        zero = jnp.zeros_like(q)
        return jnp.concatenate([jnp.where(lane // HEAD_DIM == h, q, zero) for h in range(heads)], axis=0)

    def store_scores(qt, kt, slot):
        z = _dot_nt(head_queries(qt), k_ref[0, tile_rows(kt), :])
        z_ref[slot] = z + bias_ref[...] if qt == kt else z

    def split_scores(slot):
        z = z_ref[slot]
        hi, lo = _split2(jnp.where(z > SOFTPLUS_LINEAR, z, jnp.log(1.0 + jnp.exp2(z * LOG2_E))))
        hl_ref[slot, :, :tile] = hi
        hl_ref[slot, :, tile:] = lo

    def weights(qt, kt, slot, c, acc):
        after = _dot(hl_ref[slot], from_key)
        if qt == kt:
            w = jnp.exp(after + z_ref[slot])
            acc = _dot(w.astype(BF16), v_ref[0, tile_rows(kt), :])
            c = after[:, 0:1]
        else:
            w = jnp.exp(after + z_ref[slot] + c)
            acc = acc + _dot(w.astype(BF16), v_ref[0, tile_rows(kt), :])
            c = c + after[:, 0:1]
        if kt == 0:
            out = acc[:tile]
            for h in range(1, heads):
                out = jnp.where(lane // HEAD_DIM == h, acc[h * tile:(h + 1) * tile], out)
            o_ref[0, tile_rows(qt), :] = out.astype(o_ref.dtype)
        return c, acc

    pairs = [(qt, kt) for qt in range(n_tiles) for kt in range(qt, -1, -1)]
    store_scores(*pairs[0], 0)
    split_scores(0)
    if len(pairs) > 1:
        store_scores(*pairs[1], 1)
    c = acc = None
    for n, pair in enumerate(pairs):
        if n + 1 < len(pairs):
            split_scores((n + 1) % 3)
        if n + 2 < len(pairs):
            store_scores(*pairs[n + 2], (n + 2) % 3)
        c, acc = weights(*pair, n % 3, c, acc)


def _sb_attention(u_sb, width, tile):
    b, s, _ = u_sb.shape
    blocks = width // LANES
    rows = (LANES // HEAD_DIM) * tile
    kernel = functools.partial(_sb_attn_kernel, scale=1.0 / math.sqrt(HEAD_DIM), tile=tile)
    seq_block = lambda col0: pl.BlockSpec((1, s, LANES), lambda bi, hp: (bi, 0, col0 + hp))
    return pl.pallas_call(
        kernel,
        grid=(b, blocks),
        in_specs=[seq_block(0), seq_block(blocks), seq_block(2 * blocks)],
        out_specs=seq_block(0),
        out_shape=jax.ShapeDtypeStruct((b, s, width), BF16),
        scratch_shapes=[pltpu.VMEM((3, rows, 2 * tile), BF16),
                        pltpu.VMEM((3, rows, tile), F32),
                        pltpu.VMEM((rows, tile), F32)],
        compiler_params=pltpu.CompilerParams(dimension_semantics=("parallel", "parallel"),
                                             vmem_limit_bytes=VMEM_LIMIT),
        name="sb_attention",
    )(u_sb, u_sb, u_sb)


def _advance(gen):
    return next(gen, None)


def _rwkv_kernel(rkv_ref, lora_ref, *refs):
    params, (o_ref, state_ref, prev_rkv_ref, prev_lora_ref) = refs[:-4], refs[-4:]
    pairs = o_ref.shape[2] // LANES

    @pl.when(pl.program_id(1) == 0)
    def _():
        state_ref[...] = jnp.zeros_like(state_ref)
        prev_rkv_ref[...] = jnp.zeros_like(prev_rkv_ref)
        prev_lora_ref[...] = jnp.zeros_like(prev_lora_ref)

    nseq = rkv_ref.shape[0]
    per = math.gcd(nseq, RW_STREAM_SEQS)
    streams = []
    for s0 in range(0, nseq, per):
        seqs = slice(s0, s0 + per)
        groups = slice(s0 * pairs, (s0 + per) * pairs)
        streams.append(_rwkv_stream(rkv_ref.at[seqs], lora_ref.at[seqs], *params, o_ref.at[seqs],
                                    state_ref.at[groups], prev_rkv_ref.at[seqs], prev_lora_ref.at[seqs]))
    while _advance(streams[0]) != "front end done":
        pass
    for i, current in enumerate(streams):
        following = streams[i + 1] if i + 1 < len(streams) else None
        running, front = True, following is not None
        while running or front:
            if running:
                running = _advance(current) is not None
            if front:
                front = _advance(following) != "front end done"


def _rwkv_stream(rkv_ref, lora_ref, mu_rkv_ref, mu_lora_ref, w0_ref, w2_ref, a0_ref, a2_ref,
                 g2_ref, kk_ref, ka_ref, rk_ref, lnw_ref, lnb_ref, o_ref,
                 state_ref, prev_rkv_ref, prev_lora_ref):
    nseq, c_len = rkv_ref.shape[0], rkv_ref.shape[1]
    width = o_ref.shape[2]
    pairs = width // LANES
    rows = nseq * c_len

    def token_shift(u_ref, prev_ref, mu_ref):
        u = u_ref[...].reshape(rows, u_ref.shape[2])
        t = lax.broadcasted_iota(jnp.int32, u.shape, 0)
        prev = pltpu.roll(u, 1, 0)
        for s in range(nseq):
            prev = jnp.where(t == s * c_len, prev_ref[s], prev)
            prev_ref[s] = u[(s + 1) * c_len - 1:(s + 1) * c_len, :]
        return u + (prev - u) * mu_ref[...]

    rkv = token_shift(rkv_ref, prev_rkv_ref, mu_rkv_ref)
    yield "front end"
    lora = token_shift(lora_ref, prev_lora_ref, mu_lora_ref)
    r = rkv[:, :width]
    k = rkv[:, width:2 * width]
    v = rkv[:, 2 * width:]
    x_wa = lora[:, :DECAY_LORA + ICLR_LORA]
    x_g = lora[:, DECAY_LORA + ICLR_LORA:]

    d_w = w0_ref[...] + _dot(jnp.tanh(x_wa).astype(BF16), w2_ref[...])
    log_decay = jax.nn.sigmoid(d_w) * -math.exp(-0.5)
    iclr = jax.nn.sigmoid(a0_ref[...] + _dot(x_wa.astype(BF16), a2_ref[...]))
    gate = _dot(jax.nn.sigmoid(x_g).astype(BF16), g2_ref[...])
    yield "front end"

    lane_i = lax.broadcasted_iota(jnp.int32, (LANES, LANES), 0)
    lane_j = lax.broadcasted_iota(jnp.int32, (LANES, LANES), 1)
    same_head = lane_i // HEAD_DIM == lane_j // HEAD_DIM
    head_ones = jnp.concatenate([same_head, same_head], axis=0).astype(BF16)

    def head_sum(x):
        return _dot(jnp.concatenate(_split2(x.reshape(-1, LANES)), axis=1), head_ones).reshape(x.shape)

    def group(x):
        return jnp.stack([x[s * c_len:(s + 1) * c_len, p * LANES:(p + 1) * LANES]
                          for s in range(nseq) for p in range(pairs)])

    def group_param(ref):
        return jnp.stack([ref[:, p * LANES:(p + 1) * LANES] for _ in range(nseq) for p in range(pairs)])

    k_eff = group(k * (1.0 + (iclr - 1.0) * ka_ref[...]))
    kk = group(k * kk_ref[...])
    kk = kk * lax.rsqrt(jnp.maximum(head_sum(kk * kk), 1e-24))
    yield "front end"

    t_i = lax.broadcasted_iota(jnp.int32, (rows, rows), 0)
    t_j = lax.broadcasted_iota(jnp.int32, (rows, rows), 1)
    tri = ((t_j <= t_i) & (t_i // c_len == t_j // c_len)).astype(BF16)
    cum_flat = sum(_dot(tri, part) for part in _split2(log_decay))
    cum = group(cum_flat)
    w_prev = group(jnp.exp(cum_flat - log_decay))
    yield "front end"
    w_t = jnp.exp(cum)
    inv_w = jnp.exp(-cum)
    w_last = w_t[:, c_len - 1:c_len, :]
    r = group(r)
    v = group(v)
    iclr = group(iclr)
    r_t = r * w_t
    a_t = -kk * w_prev
    b_t = kk * iclr * inv_w
    k_t = k_eff * inv_w
    yield "front end done"

    row = lax.broadcasted_iota(jnp.int32, (c_len, LANES), 0)
    col = lax.broadcasted_iota(jnp.int32, (c_len, LANES), 1) % HEAD_DIM
    strict = col < row
    incl = col <= row
    eye = (col == row).astype(F32)

    def bd(x):
        m = lax.broadcasted_iota(jnp.int32, x.shape, 2) < HEAD_DIM
        zero = jnp.zeros_like(x)
        return jnp.concatenate([jnp.where(m, x, zero), jnp.where(m, zero, x)], axis=1)

    lhs = jnp.concatenate([a_t, r_t], axis=1).astype(BF16)
    rhs = jnp.concatenate([bd(b_t.astype(BF16)), bd(k_t.astype(BF16))], axis=1)
    a_all = _bdot_nt(lhs, rhs)
    a_ab = jnp.where(strict, a_all[:, :c_len, :LANES], 0.0)
    a_ak = jnp.where(strict, a_all[:, :c_len, LANES:], 0.0)
    a_rb = jnp.where(incl, a_all[:, c_len:, :LANES], 0.0)
    a_rk = jnp.where(incl, a_all[:, c_len:, LANES:], 0.0)
    yield "chain"

    t_inv = eye + a_ab
    power = a_ab.astype(BF16)
    power = _bdot(power, bd(power))
    yield "chain"
    n = 4
    while n < c_len:
        both = _bdot(jnp.concatenate([t_inv, power], axis=1).astype(BF16), bd(power.astype(BF16)))
        t_inv = t_inv + both[:, :c_len]
        power = both[:, c_len:]
        yield "chain"
        n *= 2
    t_inv = t_inv + _bdot(t_inv.astype(BF16), bd(power.astype(BF16)))
    yield "chain"

    state = state_ref[...]
    bd_v = bd(v.astype(BF16))
    from_state = _bdot_nt(lhs, state.astype(BF16))
    yield "chain"
    u = _bdot(t_inv.astype(BF16),
              bd((from_state[:, :c_len] + _bdot(a_ak.astype(BF16), bd_v)).astype(BF16)))
    yield "chain"
    y = from_state[:, c_len:] + _bdot(
        jnp.concatenate([a_rb, a_rk], axis=2).astype(BF16),
        jnp.concatenate([bd(u.astype(BF16)), bd_v], axis=1))
    yield "chain"
    uv = jnp.concatenate([u, v], axis=1)
    bk = jnp.concatenate([b_t * w_last, k_t * w_last], axis=1)
    grown = _bdot(jnp.swapaxes(uv, 1, 2).astype(BF16), bk.astype(BF16))
    state_ref[...] = state * w_last + jnp.where(same_head, grown, 0.0)
    yield "chain"

    mean = head_sum(y) * (1.0 / HEAD_DIM)
    d = y - mean
    var = head_sum(d * d) * (1.0 / HEAD_DIM)
    y = d * lax.rsqrt(var + GN_EPS) * group_param(lnw_ref) + group_param(lnb_ref)
    y = y + head_sum(r * k_eff * group_param(rk_ref)) * v
    out = (y * group(gate)).astype(o_ref.dtype)
    for s in range(nseq):
        for p in range(pairs):
            o_ref[s, :, p * LANES:(p + 1) * LANES] = out[s * pairs + p]


def _rwkv(u_rkv, u_lora, params, width, nseq):
    b, s, _ = u_rkv.shape
    c_len = RW_CHUNK
    chunk = lambda n: pl.BlockSpec((nseq, c_len, n), lambda bi, ci: (bi, ci, 0))
    return pl.pallas_call(
        _rwkv_kernel,
        grid=(b // nseq, s // c_len),
        in_specs=[chunk(u_rkv.shape[2]), chunk(u_lora.shape[2])] + [_resident(a.shape) for a in params],
        out_specs=chunk(width),
        out_shape=jax.ShapeDtypeStruct((b, s, width), BF16),
        scratch_shapes=[pltpu.VMEM((nseq * (width // LANES), LANES, LANES), F32),
                        pltpu.VMEM((nseq, 1, u_rkv.shape[2]), F32),
                        pltpu.VMEM((nseq, 1, u_lora.shape[2]), F32)],
        compiler_params=pltpu.CompilerParams(dimension_semantics=("parallel", "arbitrary"),
                                             vmem_limit_bytes=VMEM_LIMIT),
        name="rwkv7",
    )(u_rkv, u_lora, *params)


def _post_kernel(x_ref, osb_ref, orw_ref, gate_ref, p_ref, wupsb_ref, wuprw_ref, wout_ref,
                 gmlp_ref, wff1_ref, wff2_ref, gple_ref, wpg_ref, wpp_ref, gfin_ref, o_ref,
                 *, ff_chunk, final):
    x = x_ref[...]
    d = x.shape[1]
    merged = (gate_ref[:, :d].astype(F32) * _dot(osb_ref[...], wupsb_ref[...])
              + gate_ref[:, d:].astype(F32) * _dot(orw_ref[...], wuprw_ref[...]))
    x = x + _dot(merged.astype(BF16), wout_ref[...])

    h = _rms(x, gmlp_ref[...]).astype(BF16)
    mlp = jnp.zeros_like(x)
    for f0 in range(0, wff1_ref.shape[1], ff_chunk):
        hid = jnp.maximum(_dot(h, wff1_ref[:, f0:f0 + ff_chunk]), 0.0)
        mlp = mlp + _dot((hid * hid).astype(BF16), wff2_ref[f0:f0 + ff_chunk, :])
    x = x + mlp

    g = jax.nn.sigmoid(_dot(_rms(x, gple_ref[...]).astype(BF16), wpg_ref[...]))
    x = x + g * _dot(p_ref[...].astype(BF16), wpp_ref[...])
    o_ref[...] = _rms(x, gfin_ref[...]) if final else x


def _post(x2, o_sb, o_rw, gates, p2, weights, tm, final):
    t, d = x2.shape
    row = lambda n: pl.BlockSpec((tm, n), lambda i: (i, 0))
    kernel = functools.partial(_post_kernel, ff_chunk=POST_FF_CHUNK, final=final)
    return pl.pallas_call(
        kernel,
        grid=(t // tm,),
        in_specs=[row(d), row(o_sb.shape[1]), row(o_rw.shape[1]), row(gates.shape[1]), row(p2.shape[1])]
                 + [_resident(w.shape) for w in weights],
        out_specs=row(d),
        out_shape=jax.ShapeDtypeStruct((t, d), F32),
        compiler_params=pltpu.CompilerParams(dimension_semantics=("parallel",),
                                             vmem_limit_bytes=VMEM_LIMIT),
        name="post",
    )(x2, o_sb, o_rw, gates, p2, *weights)


def kernel(x, p, attn_norm_g, w_in, shift_mu, decay_w0, decay_w2, iclr_a0, iclr_a2, gate_g2, k_k, k_a, r_k, ln_x_w, ln_x_b, w_up_sb, w_up_rw, w_out, mlp_norm_g, w_ff1, w_ff2, ple_norm_g, w_ple_gate, w_ple_proj, final_norm_g):
    bsz, seq, d = x.shape
    depth = w_in.shape[0]
    sb_width = w_up_sb.shape[1]
    rw_width = w_up_rw.shape[1]
    sb_cols = 3 * sb_width
    rkv_cols = 3 * rw_width
    lora_cols = DECAY_LORA + ICLR_LORA + GATE_LORA
    rw_end = sb_cols + rkv_cols + lora_cols
    tokens = bsz * seq
    tm_in = min(IN_ROWS, tokens)
    tm_post = min(POST_ROWS, tokens)
    vec = lambda a: a.reshape(1, -1).astype(F32)
    bf = lambda a: a.astype(BF16)

    x2 = x.reshape(tokens, d)
    for i in range(depth):
        wi = w_in[i]
        u_sb, u_rkv, u_lora, gates = _inproj(
            x2, vec(attn_norm_g[i]), bf(wi[:, :sb_cols]), bf(wi[:, sb_cols:sb_cols + rkv_cols]),
            bf(wi[:, sb_cols + rkv_cols:rw_end]), bf(wi[:, rw_end:]), tm_in)

        o_sb = _sb_attention(u_sb.reshape(bsz, seq, sb_cols), sb_width, min(SB_TILE, seq))

        mu = shift_mu[i]
        w2_pad = jnp.concatenate([decay_w2[i], jnp.zeros((ICLR_LORA, rw_width), F32)], axis=0)
        a2_pad = jnp.concatenate([jnp.zeros((DECAY_LORA, rw_width), F32), iclr_a2[i]], axis=0)
        rw_params = (
            vec(mu[:rkv_cols]), vec(mu[rkv_cols:]), vec(decay_w0[i]), bf(w2_pad), vec(iclr_a0[i]),
            bf(a2_pad), bf(gate_g2[i]), vec(k_k[i]), vec(k_a[i]), vec(r_k[i]), vec(ln_x_w[i]),
            vec(ln_x_b[i]))
        o_rw = _rwkv(u_rkv.reshape(bsz, seq, rkv_cols), u_lora.reshape(bsz, seq, lora_cols),
                     rw_params, rw_width, math.gcd(bsz, RW_SEQS))

        post_weights = (bf(w_up_sb[i]), bf(w_up_rw[i]), bf(w_out[i]), vec(mlp_norm_g[i]),
                        bf(w_ff1[i]), bf(w_ff2[i]), vec(ple_norm_g[i]), bf(w_ple_gate[i]),
                        bf(w_ple_proj[i]), vec(final_norm_g))
        x2 = _post(x2, o_sb.reshape(tokens, sb_width), o_rw.reshape(tokens, rw_width), gates,
                   p[i].reshape(tokens, -1), post_weights, tm_post, final=(i == depth - 1))
    return x2.reshape(bsz, seq, d)
```
